```python
import jax, jax.numpy as jnp
from jax import lax
import numpy as np

D_MODEL = 1024
BATCH = 4
SEQ = 8192
DEPTH = 2

CHUNK = 64
N_MIXERS = 2
N_FOX = (DEPTH + 1) // 2
N_SGU = DEPTH // 2
FOX_HEADS = 16
FOX_HEAD_DIM = D_MODEL // FOX_HEADS
Q_BLOCK = 128
SGU_WIDTH = 2 * D_MODEL
SGU_GROUPS = 8
SGU_GROUP_DIM = SGU_WIDTH // SGU_GROUPS
SGU_BLOCK = 128
D_FF = 2816
CONV_WIDTH = 3
EPS = 1e-6

kernel_name = "fox_gmlp_convffn_adaln_hybrid"


def rmsnorm(x, g):
    xf = x.astype(jnp.float32)
    y = xf * lax.rsqrt(jnp.mean(xf * xf, axis=-1, keepdims=True) + EPS)
    return (y * g.astype(jnp.float32)).astype(x.dtype)


def layernorm(x, g, b):
    xf = x.astype(jnp.float32)
    mu = jnp.mean(xf, axis=-1, keepdims=True)
    var = jnp.mean(jnp.square(xf - mu), axis=-1, keepdims=True)
    y = (xf - mu) * lax.rsqrt(var + EPS)
    return (y * g.astype(jnp.float32) + b.astype(jnp.float32)).astype(x.dtype)


def modulate(h, shift, scale):
    return h * (1 + scale[:, None, :]) + shift[:, None, :]


def forgetting_attention(h, w_in, b_f, q_gain, k_gain, w_out):
    B, S, D = h.shape
    H, Dh = FOX_HEADS, FOX_HEAD_DIM
    proj = h @ w_in
    q, k, v, o, fl = jnp.split(proj, [D, 2 * D, 3 * D, 4 * D], axis=-1)
    q = rmsnorm(q.reshape(B, S, H, Dh), q_gain)
    k = rmsnorm(k.reshape(B, S, H, Dh), k_gain)
    v = v.reshape(B, S, H, Dh)
    logf = jax.nn.log_sigmoid((fl + b_f).astype(jnp.float32))
    F = jnp.cumsum(logf, axis=1).transpose(0, 2, 1)
    scale = Dh ** -0.5
    outs = []
    for qb in range(S // Q_BLOCK):
        q0, q1 = qb * Q_BLOCK, (qb + 1) * Q_BLOCK
        logits = jnp.einsum('bqhd,bkhd->bhqk', q[:, q0:q1], k[:, :q1]).astype(jnp.float32) * scale
        decay = F[:, :, q0:q1, None] - F[:, :, None, :q1]
        qpos = jnp.arange(q0, q1)[:, None]
        kpos = jnp.arange(q1)[None, :]
        logits = jnp.where(kpos <= qpos, logits + decay, -jnp.inf)
        p = jax.nn.softmax(logits, axis=-1).astype(v.dtype)
        outs.append(jnp.einsum('bhqk,bkhd->bqhd', p, v[:, :q1]))
    att = jnp.concatenate(outs, axis=1).reshape(B, S, D)
    return (att * jax.nn.sigmoid(o)) @ w_out


def spatial_gating_mlp(h, w_in, b_in, v_gain, v_bias, w_s, b_s, w_out):
    B, S, _ = h.shape
    z = jax.nn.gelu(h @ w_in + b_in)
    u, v = jnp.split(z, 2, axis=-1)
    v = layernorm(v, v_gain, v_bias)
    n = S // SGU_BLOCK
    v = v.reshape(B, n, SGU_BLOCK, SGU_GROUPS, SGU_GROUP_DIM)
    t = jnp.arange(SGU_BLOCK)
    mask = (t[None, :] // CHUNK) <= (t[:, None] // CHUNK)
    ws = jnp.where(mask[None], w_s, 0)
    mixed = jnp.einsum('gts,bnsgc->bntgc', ws, v) + b_s.T[None, None, :, :, None]
    y = u * mixed.reshape(B, S, SGU_WIDTH)
    return y @ w_out


def conv_gated_ffn(h, w_up, conv_w, conv_b, w_down):
    S = h.shape[1]
    a = h @ w_up
    ap = jnp.pad(a, ((0, 0), (CONV_WIDTH - 1, 0), (0, 0)))
    acc = ap[:, 0:S] * conv_w[0]
    for j in range(1, CONV_WIDTH):
        acc = acc + ap[:, j:j + S] * conv_w[j]
    a = acc + conv_b
    g, val = jnp.split(a, 2, axis=-1)
    return (jax.nn.silu(g) * val) @ w_down


def setup_inputs(seed: int = 0) -> dict:
    key = jax.random.key(seed)
    ks = jax.random.split(key, 24)
    D = D_MODEL
    nrm = jax.random.normal
    f32 = jnp.float32
    return {
        "x": nrm(ks[0], (BATCH, SEQ, D), f32),
        "c": nrm(ks[1], (BATCH, D), f32),
        "fox_w_in": nrm(ks[2], (N_FOX, D, 4 * D + FOX_HEADS), f32) * D ** -0.5,
        "fox_b_f": 3.0 + 0.5 * nrm(ks[3], (N_FOX, FOX_HEADS), f32),
        "fox_q_gain": 1.0 + 0.02 * nrm(ks[4], (N_FOX, FOX_HEAD_DIM), f32),
        "fox_k_gain": 1.0 + 0.02 * nrm(ks[5], (N_FOX, FOX_HEAD_DIM), f32),
        "fox_w_out": nrm(ks[6], (N_FOX, D, D), f32) * D ** -0.5,
        "sgu_w_in": nrm(ks[7], (N_SGU, D, 2 * SGU_WIDTH), f32) * D ** -0.5,
        "sgu_b_in": 0.02 * nrm(ks[8], (N_SGU, 2 * SGU_WIDTH), f32),
        "sgu_v_gain": 1.0 + 0.02 * nrm(ks[9], (N_SGU, SGU_WIDTH), f32),
        "sgu_v_bias": 0.02 * nrm(ks[10], (N_SGU, SGU_WIDTH), f32),
        "sgu_w_s": nrm(ks[11], (N_SGU, SGU_GROUPS, SGU_BLOCK, SGU_BLOCK), f32) * SGU_BLOCK ** -0.5,
        "sgu_b_s": 1.0 + 0.1 * nrm(ks[12], (N_SGU, SGU_GROUPS, SGU_BLOCK), f32),
        "sgu_w_out": nrm(ks[13], (N_SGU, SGU_WIDTH, D), f32) * SGU_WIDTH ** -0.5,
        "ffn_w_up": nrm(ks[14], (DEPTH, D, 2 * D_FF), f32) * D ** -0.5,
        "ffn_conv_w": nrm(ks[15], (DEPTH, CONV_WIDTH, 2 * D_FF), f32) * CONV_WIDTH ** -0.5,
        "ffn_conv_b": 0.02 * nrm(ks[16], (DEPTH, 2 * D_FF), f32),
        "ffn_w_down": nrm(ks[17], (DEPTH, D_FF, D), f32) * D_FF ** -0.5,
        "ada_w": nrm(ks[18], (DEPTH, D, 6 * D), f32) * (0.5 * D ** -0.5),
        "ada_b": 0.02 * nrm(ks[19], (DEPTH, 6 * D), f32),
        "norm1_g": 1.0 + 0.02 * nrm(ks[20], (DEPTH, D), f32),
        "norm2_g": 1.0 + 0.02 * nrm(ks[21], (DEPTH, D), f32),
        "final_g": 1.0 + 0.02 * nrm(ks[22], (D,), f32),
    }


def reference(x, c, fox_w_in, fox_b_f, fox_q_gain, fox_k_gain, fox_w_out,
              sgu_w_in, sgu_b_in, sgu_v_gain, sgu_v_bias, sgu_w_s, sgu_b_s, sgu_w_out,
              ffn_w_up, ffn_conv_w, ffn_conv_b, ffn_w_down,
              ada_w, ada_b, norm1_g, norm2_g, final_g):
    c_act = jax.nn.silu(c)
    for i in range(DEPTH):
        mod = c_act @ ada_w[i] + ada_b[i]
        sh1, sc1, g1, sh2, sc2, g2 = jnp.split(mod, 6, axis=-1)
        h = modulate(rmsnorm(x, norm1_g[i]), sh1, sc1)
        j = i // N_MIXERS
        if i % N_MIXERS == 0:
            y = forgetting_attention(h, fox_w_in[j], fox_b_f[j], fox_q_gain[j],
                                     fox_k_gain[j], fox_w_out[j])
        else:
            y = spatial_gating_mlp(h, sgu_w_in[j], sgu_b_in[j], sgu_v_gain[j],
                                   sgu_v_bias[j], sgu_w_s[j], sgu_b_s[j], sgu_w_out[j])
        x = x + g1[:, None, :] * y
        h = modulate(rmsnorm(x, norm2_g[i]), sh2, sc2)
        x = x + g2[:, None, :] * conv_gated_ffn(h, ffn_w_up[i], ffn_conv_w[i],
                                                ffn_conv_b[i], ffn_w_down[i])
    return rmsnorm(x, final_g)
```

```python
import functools

import jax
import jax.numpy as jnp
from jax import lax
from jax.experimental import pallas as pl
from jax.experimental.pallas import tpu as pltpu

F32 = jnp.float32
BF16 = jnp.bfloat16

FOX_HEADS = 16
FOX_HEAD_DIM = 64
SGU_GROUPS = 8
SGU_BLOCK = 128
CHUNK = 64
CONV_WIDTH = 3
EPS = 1e-6

LANES = 128
SUBLANES = 8
VMEM_LIMIT = 56 * 1024 * 1024
NEG_BIG = -1e30


def _cparams(sem):
    return pltpu.CompilerParams(dimension_semantics=sem, vmem_limit_bytes=VMEM_LIMIT)


def _const_spec(shape):
    nd = len(shape)
    return pl.BlockSpec(shape, lambda *_: (0,) * nd, pipeline_mode=pl.Buffered(1))


def _sigmoid(x):
    return 1.0 / (1.0 + jnp.exp(-x))


def _norm_mod(x, g, sh, sc):
    ms = jnp.mean(x * x, axis=-1, keepdims=True)
    h = x * lax.rsqrt(ms + EPS) * g
    return h * (1.0 + sc) + sh


def _mod_kernel(c_ref, w_ref, b_ref, o_ref):
    c = c_ref[...]
    ca = c * _sigmoid(c)
    o_ref[0] = jnp.dot(ca, w_ref[0], preferred_element_type=F32,
                       precision=lax.Precision.HIGHEST) + b_ref[0]


def _ada_mod(c, ada_w, ada_b):
    depth, d, n = ada_w.shape
    bsz = c.shape[0]
    rows = -(-bsz // SUBLANES) * SUBLANES
    cp = jnp.zeros((rows, d), F32).at[:bsz].set(c)
    tn = 1024
    out = pl.pallas_call(
        _mod_kernel,
        grid=(depth, n // tn),
        in_specs=[
            pl.BlockSpec((rows, d), lambda l, j: (0, 0)),
            pl.BlockSpec((1, d, tn), lambda l, j: (l, 0, j)),
            pl.BlockSpec((1, 1, tn), lambda l, j: (l, 0, j)),
        ],
        out_specs=pl.BlockSpec((1, rows, tn), lambda l, j: (l, 0, j)),
        out_shape=jax.ShapeDtypeStruct((depth, rows, n), F32),
        compiler_params=_cparams(("parallel", "parallel")),
        name="ada_mod",
    )(cp, ada_w, ada_b.reshape(depth, 1, n))
    return out[:, :bsz]


def _fox_in_kernel(x_ref, g_ref, sh_ref, sc_ref, w_ref, wfl_ref, o_ref, fl_ref, *, tn):
    hb = _norm_mod(x_ref[0], g_ref[...], sh_ref[0], sc_ref[0]).astype(BF16)
    n = w_ref.shape[1]
    for j in range(0, n, tn):
        o_ref[0, :, j:j + tn] = jnp.dot(
            hb, w_ref[:, j:j + tn], preferred_element_type=F32).astype(BF16)
    fl_ref[0] = jnp.dot(hb, wfl_ref[...], preferred_element_type=F32)


def _fox_in(x, g, sh, sc, w, wfl, tm):
    bsz, s, d = x.shape
    n = w.shape[1]
    kern = functools.partial(_fox_in_kernel, tn=1024)
    return pl.pallas_call(
        kern,
        grid=(bsz, s // tm),
        in_specs=[
            pl.BlockSpec((1, tm, d), lambda b, i: (b, i, 0)),
            _const_spec((1, d)),
            pl.BlockSpec((1, 1, d), lambda b, i: (b, 0, 0)),
            pl.BlockSpec((1, 1, d), lambda b, i: (b, 0, 0)),
            _const_spec((d, n)),
            _const_spec((d, LANES)),
        ],
        out_specs=[
            pl.BlockSpec((1, tm, n), lambda b, i: (b, i, 0)),
            pl.BlockSpec((1, tm, LANES), lambda b, i: (b, i, 0)),
        ],
        out_shape=[
            jax.ShapeDtypeStruct((bsz, s, n), BF16),
            jax.ShapeDtypeStruct((bsz, s, LANES), F32),
        ],
        compiler_params=_cparams(("parallel", "parallel")),
        name="fox_in",
    )(x, g, sh, sc, w, wfl)


def _fcum_kernel(fl_ref, bf_ref, o_ref):
    z = fl_ref[0] + bf_ref[0]
    v = jnp.minimum(z, 0.0) - jnp.log(1.0 + jnp.exp(-jnp.abs(z)))
    r = v.shape[0]
    lane = lax.broadcasted_iota(jnp.int32, v.shape, 1)
    row = lax.broadcasted_iota(jnp.int32, v.shape, 0)
    d = 1
    while d < LANES:
        v = v + jnp.where(lane >= d, pltpu.roll(v, d, axis=1), 0.0)
        d *= 2
    tot = jnp.broadcast_to(v[:, LANES - 1:LANES], v.shape)
    inc = tot
    d = 1
    while d < r:
        inc = inc + jnp.where(row >= d, pltpu.roll(inc, d, axis=0), 0.0)
        d *= 2
    o_ref[0] = v + (inc - tot)


def _fcum(fl_t, bf_rows):
    bh, r, _ = fl_t.shape
    return pl.pallas_call(
        _fcum_kernel,
        grid=(bh,),
        in_specs=[
            pl.BlockSpec((1, r, LANES), lambda i: (i, 0, 0)),
            pl.BlockSpec((1, 1, LANES), lambda i: (i, 0, 0)),
        ],
        out_specs=pl.BlockSpec((1, r, LANES), lambda i: (i, 0, 0)),
        out_shape=jax.ShapeDtypeStruct((bh, r, LANES), F32),
        compiler_params=_cparams(("parallel",)),
        name="fcum",
    )(fl_t, bf_rows)


def _head_norm(x, gain, lane):
    x2 = x * x
    lo = lane < FOX_HEAD_DIM
    ss_a = jnp.sum(jnp.where(lo, x2, 0.0), axis=-1, keepdims=True)
    ss_b = jnp.sum(jnp.where(lo, 0.0, x2), axis=-1, keepdims=True)
    ms = jnp.where(lo, ss_a, ss_b) * (1.0 / FOX_HEAD_DIM)
    return x * lax.rsqrt(ms + EPS) * gain


def _attn_kernel(q_ref, k_ref, v_ref, o_ref, f_ref, qg_ref, kg_ref, out_ref,
                 knt_ref, va_ref, vb_ref, *, tq, prep_rows):
    qi = pl.program_id(2)
    s_len = k_ref.shape[1]

    @pl.when(qi == 0)
    def _prepare_kv():
        lane = lax.broadcasted_iota(jnp.int32, (prep_rows, LANES), 1)
        lo = lane < FOX_HEAD_DIM

        def body(c, carry):
            r0 = pl.multiple_of(c * prep_rows, prep_rows)
            kn = _head_norm(k_ref[0, pl.ds(r0, prep_rows), :].astype(F32), kg_ref[...], lane)
            knt_ref[:, pl.ds(r0, prep_rows)] = kn.T.astype(BF16)
            v = v_ref[0, pl.ds(r0, prep_rows), :]
            one = jnp.ones_like(v)
            va_ref[pl.ds(r0, prep_rows), :] = jnp.where(lo, v, one)
            vb_ref[pl.ds(r0, prep_rows), :] = jnp.where(lo, one, v)
            return carry

        lax.fori_loop(0, s_len // prep_rows, body, 0)

    lane = lax.broadcasted_iota(jnp.int32, (tq, LANES), 1)
    lo = lane < FOX_HEAD_DIM
    qn = _head_norm(q_ref[0].astype(F32), qg_ref[...], lane) * (FOX_HEAD_DIM ** -0.5)
    q0 = pl.multiple_of(qi * tq, tq)
    row = lax.broadcasted_iota(jnp.int32, (tq, tq), 0)
    col = lax.broadcasted_iota(jnp.int32, (tq, tq), 1)
    causal = col <= row

    def one_head(hh, qh, vh_ref):
        f_diag = f_ref[0, 0, hh:hh + 1, pl.ds(q0, tq)]
        f0 = f_diag[:, 0:1]

        def block(c0, bias, carry, mask):
            m, acc = carry
            s = jnp.dot(qh, knt_ref[:, pl.ds(c0, tq)], preferred_element_type=F32) + bias
            if mask:
                s = jnp.where(causal, s, NEG_BIG)
            m_new = jnp.maximum(m, jnp.max(s, axis=-1, keepdims=True))
            p = jnp.exp(s - m_new)
            alpha = jnp.exp(m - m_new)
            pv = jnp.dot(p.astype(BF16), vh_ref[pl.ds(c0, tq), :], preferred_element_type=F32)
            return m_new, acc * alpha + pv

        def body(kb, carry):
            c0 = pl.multiple_of(kb * tq, tq)
            bias = f0 - f_ref[0, 0, hh:hh + 1, pl.ds(c0, tq)]
            return block(c0, bias, carry, False)

        init = (jnp.full((tq, 1), NEG_BIG, F32), jnp.zeros((tq, LANES), F32))
        carry = lax.fori_loop(0, qi, body, init)
        _, acc = block(q0, f0 - f_diag, carry, True)
        return acc / pltpu.roll(acc, FOX_HEAD_DIM, axis=1)

    zero = jnp.zeros_like(qn)
    res_a = one_head(0, jnp.where(lo, qn, zero).astype(BF16), va_ref)
    res_b = one_head(1, jnp.where(lo, zero, qn).astype(BF16), vb_ref)
    att = jnp.where(lo, res_a, res_b)
    out_ref[0] = (att * _sigmoid(o_ref[0].astype(F32))).astype(BF16)


def _attention(qkvo, f_pairs, qg2, kg2, tq):
    bsz, s, n = qkvo.shape
    d = n // 4
    npairs = d // LANES
    kern = functools.partial(_attn_kernel, tq=tq, prep_rows=min(512, s))
    return pl.pallas_call(
        kern,
        grid=(bsz, npairs, s // tq),
        in_specs=[
            pl.BlockSpec((1, tq, LANES), lambda b, h, i: (b, i, h)),
            pl.BlockSpec((1, s, LANES), lambda b, h, i: (b, 0, npairs + h)),
            pl.BlockSpec((1, s, LANES), lambda b, h, i: (b, 0, 2 * npairs + h)),
            pl.BlockSpec((1, tq, LANES), lambda b, h, i: (b, i, 3 * npairs + h)),
            pl.BlockSpec((1, 1, 2, s), lambda b, h, i: (b, h, 0, 0)),
            _const_spec((1, LANES)),
            _const_spec((1, LANES)),
        ],
        out_specs=pl.BlockSpec((1, tq, LANES), lambda b, h, i: (b, i, h)),
        out_shape=jax.ShapeDtypeStruct((bsz, s, d), BF16),
        scratch_shapes=[
            pltpu.VMEM((LANES, s), BF16),
            pltpu.VMEM((s, LANES), BF16),
            pltpu.VMEM((s, LANES), BF16),
        ],
        compiler_params=_cparams(("parallel", "parallel", "arbitrary")),
        name="fox_attn",
    )(qkvo, qkvo, qkvo, qkvo, f_pairs, qg2, kg2)


def _shift_rows(a, d, prev_tail):
    r = pltpu.roll(a, d, axis=0)
    row = lax.broadcasted_iota(jnp.int32, prev_tail.shape, 0)
    top = jnp.where(row < d, pltpu.roll(prev_tail, d, axis=0), r[:SUBLANES])
    return jnp.concatenate([top, r[SUBLANES:]], axis=0)


def _ffn_kernel(*refs, cw, fuse_attn_out, final_norm):
    refs = list(refs)
    x_ref = refs.pop(0)
    if fuse_attn_out:
        ao_ref, wo_ref, g1_ref = refs.pop(0), refs.pop(0), refs.pop(0)
    g_ref, sh_ref, sc_ref, g2_ref, wup_ref, cw_ref, cb_ref, wdn_ref = refs[:8]
    refs = refs[8:]
    if final_norm:
        fg_ref = refs.pop(0)
    o_ref, tail_ref, act_ref = refs

    si = pl.program_id(1)
    tm = x_ref.shape[1]
    dff = wdn_ref.shape[0]

    @pl.when(si == 0)
    def _zero_tail():
        tail_ref[...] = jnp.zeros_like(tail_ref)

    x = x_ref[0]
    if fuse_attn_out:
        x = x + g1_ref[0] * jnp.dot(ao_ref[0], wo_ref[...], preferred_element_type=F32)
    hb = _norm_mod(x, g_ref[...], sh_ref[0], sc_ref[0]).astype(BF16)

    def conv(c0):
        a = jnp.dot(hb, wup_ref[:, c0:c0 + cw], preferred_element_type=F32)
        prev = tail_ref[:, c0:c0 + cw]
        tail_ref[:, c0:c0 + cw] = a[tm - SUBLANES:]
        w = cw_ref[:, c0:c0 + cw]
        return (_shift_rows(a, 2, prev) * w[0:1] + _shift_rows(a, 1, prev) * w[1:2]
                + a * w[2:3] + cb_ref[:, c0:c0 + cw])

    for c0 in range(0, dff, cw):
        gate = conv(c0)
        val = conv(dff + c0)
        act_ref[:, c0:c0 + cw] = (gate * _sigmoid(gate) * val).astype(BF16)

    y = jnp.dot(act_ref[...], wdn_ref[...], preferred_element_type=F32)
    xo = x + g2_ref[0] * y
    if final_norm:
        ms = jnp.mean(xo * xo, axis=-1, keepdims=True)
        xo = xo * lax.rsqrt(ms + EPS) * fg_ref[...]
    o_ref[0] = xo


def _ffn(x, attn_out, norm_g, sh, sc, g2, w_up, conv_w, conv_b, w_down, final_g, tm):
    bsz, s, d = x.shape
    dff = w_down.shape[0]
    fuse_attn_out = attn_out is not None
    final_norm = final_g is not None
    row_spec = pl.BlockSpec((1, tm, d), lambda b, i: (b, i, 0))
    vec_spec = pl.BlockSpec((1, 1, d), lambda b, i: (b, 0, 0))
    args, specs = [x], [row_spec]
    if fuse_attn_out:
        ao, w_out, g1 = attn_out
        args += [ao, w_out, g1]
        specs += [row_spec, _const_spec((d, d)), vec_spec]
    args += [norm_g, sh, sc, g2, w_up, conv_w, conv_b, w_down]
    specs += [_const_spec((1, d)), vec_spec, vec_spec, vec_spec,
              _const_spec((d, 2 * dff)), _const_spec((CONV_WIDTH, 2 * dff)),
              _const_spec((1, 2 * dff)), _const_spec((dff, d))]
    if final_norm:
        args.append(final_g)
        specs.append(_const_spec((1, d)))
    kern = functools.partial(_ffn_kernel, cw=256, fuse_attn_out=fuse_attn_out,
                             final_norm=final_norm)
    return pl.pallas_call(
        kern,
        grid=(bsz, s // tm),
        in_specs=specs,
        out_specs=row_spec,
        out_shape=jax.ShapeDtypeStruct((bsz, s, d), F32),
        scratch_shapes=[
            pltpu.VMEM((SUBLANES, 2 * dff), F32),
            pltpu.VMEM((tm, dff), BF16),
        ],
        compiler_params=_cparams(("parallel", "arbitrary")),
        name="conv_ffn",
    )(*args)


def _gelu_tanh(x):
    return 0.5 * x * (1.0 + jnp.tanh(0.7978845608028654 * (x + 0.044715 * (x * x * x))))


def _sgu_kernel(x_ref, g_ref, sh_ref, sc_ref, g1_ref, win_ref, bin_ref, vg_ref, vb_ref,
                ws_ref, bs_ref, wout_ref, o_ref, v_ref, vn_ref, y_ref):
    tm = x_ref.shape[1]
    width = vg_ref.shape[1]
    gdim = width // SGU_GROUPS
    x = x_ref[0]
    hb = _norm_mod(x, g_ref[...], sh_ref[0], sc_ref[0]).astype(BF16)

    def z_cols(c0, n):
        return _gelu_tanh(jnp.dot(hb, win_ref[:, c0:c0 + n], preferred_element_type=F32)
                          + bin_ref[:, c0:c0 + n])

    vcw = 512
    for c0 in range(0, width, vcw):
        v_ref[:, c0:c0 + vcw] = z_cols(width + c0, vcw)
    v = v_ref[...]
    mu = jnp.mean(v, axis=-1, keepdims=True)
    dv = v - mu
    var = jnp.mean(dv * dv, axis=-1, keepdims=True)
    vn_ref[...] = (dv * lax.rsqrt(var + EPS) * vg_ref[...] + vb_ref[...]).astype(BF16)

    t = lax.broadcasted_iota(jnp.int32, (SGU_BLOCK, SGU_BLOCK), 0)
    s = lax.broadcasted_iota(jnp.int32, (SGU_BLOCK, SGU_BLOCK), 1)
    allowed = (s // CHUNK) <= (t // CHUNK)
    for g in range(SGU_GROUPS):
        c0 = g * gdim
        u = z_cols(c0, gdim)
        ws = jnp.where(allowed, ws_ref[g], 0.0).astype(BF16)
        bias = jnp.concatenate([bs_ref[g]] * (gdim // LANES), axis=1)
        for r0 in range(0, tm, SGU_BLOCK):
            mixed = jnp.dot(ws, vn_ref[r0:r0 + SGU_BLOCK, c0:c0 + gdim],
                            preferred_element_type=F32) + bias
            y_ref[r0:r0 + SGU_BLOCK, c0:c0 + gdim] = (u[r0:r0 + SGU_BLOCK] * mixed).astype(BF16)

    y = jnp.dot(y_ref[...], wout_ref[...], preferred_element_type=F32)
    o_ref[0] = x + g1_ref[0] * y


def _sgu(x, norm_g, sh, sc, g1, w_in, b_in, v_gain, v_bias, w_s, bs_b, w_out, tm):
    bsz, s, d = x.shape
    width = w_out.shape[0]
    row_spec = pl.BlockSpec((1, tm, d), lambda b, i: (b, i, 0))
    vec_spec = pl.BlockSpec((1, 1, d), lambda b, i: (b, 0, 0))
    return pl.pallas_call(
        _sgu_kernel,
        grid=(bsz, s // tm),
        in_specs=[
            row_spec, _const_spec((1, d)), vec_spec, vec_spec, vec_spec,
            _const_spec((d, 2 * width)), _const_spec((1, 2 * width)),
            _const_spec((1, width)), _const_spec((1, width)),
            _const_spec((SGU_GROUPS, SGU_BLOCK, SGU_BLOCK)),
            _const_spec((SGU_GROUPS, SGU_BLOCK, LANES)),
            _const_spec((width, d)),
        ],
        out_specs=row_spec,
        out_shape=jax.ShapeDtypeStruct((bsz, s, d), F32),
        scratch_shapes=[
            pltpu.VMEM((tm, width), F32),
            pltpu.VMEM((tm, width), BF16),
            pltpu.VMEM((tm, width), BF16),
        ],
        compiler_params=_cparams(("parallel", "parallel")),
        name="sgu",
    )(x, norm_g, sh, sc, g1, w_in, b_in, v_gain, v_bias, w_s, bs_b, w_out)


def kernel(x, c, fox_w_in, fox_b_f, fox_q_gain, fox_k_gain, fox_w_out, sgu_w_in, sgu_b_in,
           sgu_v_gain, sgu_v_bias, sgu_w_s, sgu_b_s, sgu_w_out, ffn_w_up, ffn_conv_w,
           ffn_conv_b, ffn_w_down, ada_w, ada_b, norm1_g, norm2_g, final_g):
    bsz, s, d = x.shape
    depth = ada_w.shape[0]
    assert depth == 2 and d == FOX_HEADS * FOX_HEAD_DIM
    assert s % 512 == 0
    tm = 512
    tq = 256

    mod = _ada_mod(c, ada_w, ada_b)
    mods = [[mod[l, :, k * d:(k + 1) * d].reshape(bsz, 1, d) for k in range(6)]
            for l in range(depth)]

    sh1, sc1, g1, sh2, sc2, g2 = mods[0]
    w_in = fox_w_in[0]
    w_main = w_in[:, :4 * d].astype(BF16)
    w_fl = jnp.zeros((d, LANES), BF16).at[:, :FOX_HEADS].set(w_in[:, 4 * d:].astype(BF16))
    qkvo, fl = _fox_in(x, norm1_g[0].reshape(1, d), sh1, sc1, w_main, w_fl, tm)

    fl_t = jnp.transpose(fl[:, :, :FOX_HEADS], (0, 2, 1)).reshape(
        bsz * FOX_HEADS, s // LANES, LANES)
    bf_rows = jnp.broadcast_to(jnp.tile(fox_b_f[0], bsz)[:, None, None],
                               (bsz * FOX_HEADS, 1, LANES))
    f_pairs = _fcum(fl_t, bf_rows).reshape(bsz, FOX_HEADS // 2, 2, s)

    qg2 = jnp.tile(fox_q_gain[0], 2).reshape(1, LANES)
    kg2 = jnp.tile(fox_k_gain[0], 2).reshape(1, LANES)
    ao = _attention(qkvo, f_pairs, qg2, kg2, tq)

    x = _ffn(x, (ao, fox_w_out[0].astype(BF16), g1), norm2_g[0].reshape(1, d), sh2, sc2, g2,
             ffn_w_up[0].astype(BF16), ffn_conv_w[0], ffn_conv_b[0].reshape(1, -1),
             ffn_w_down[0].astype(BF16), None, tm)

    sh1, sc1, g1, sh2, sc2, g2 = mods[1]
    width = sgu_w_out.shape[1]
    bs_b = jnp.broadcast_to(sgu_b_s[0][:, :, None], (SGU_GROUPS, SGU_BLOCK, LANES))
    x = _sgu(x, norm1_g[1].reshape(1, d), sh1, sc1, g1, sgu_w_in[0].astype(BF16),
             sgu_b_in[0].reshape(1, -1), sgu_v_gain[0].reshape(1, width),
             sgu_v_bias[0].reshape(1, width), sgu_w_s[0], bs_b, sgu_w_out[0].astype(BF16), tm)
    x = _ffn(x, None, norm2_g[1].reshape(1, d), sh2, sc2, g2,
             ffn_w_up[1].astype(BF16), ffn_conv_w[1], ffn_conv_b[1].reshape(1, -1),
             ffn_w_down[1].astype(BF16), final_g.reshape(1, d), tm)
    return x
```

```python
import functools

import jax
import jax.numpy as jnp
from jax import lax
from jax.experimental import pallas as pl
from jax.experimental.pallas import tpu as pltpu

F32 = jnp.float32
BF16 = jnp.bfloat16

FOX_HEADS = 16
FOX_HEAD_DIM = 64
SGU_GROUPS = 8
SGU_BLOCK = 128
CHUNK = 64
CONV_WIDTH = 3
EPS = 1e-6

LANES = 128
SUBLANES = 8
VMEM_LIMIT = 56 * 1024 * 1024
NEG_BIG = -1e30


def _cparams(sem):
    return pltpu.CompilerParams(dimension_semantics=sem, vmem_limit_bytes=VMEM_LIMIT)


def _const_spec(shape):
    nd = len(shape)
    return pl.BlockSpec(shape, lambda *_: (0,) * nd, pipeline_mode=pl.Buffered(1))


def _sigmoid(x):
    return 1.0 / (1.0 + jnp.exp(-x))


def _norm_mod(x, g, sh, sc):
    ms = jnp.mean(x * x, axis=-1, keepdims=True)
    h = x * lax.rsqrt(ms + EPS) * g
    return h * (1.0 + sc) + sh


def _mod_kernel(c_ref, w_ref, b_ref, o_ref):
    c = c_ref[...]
    ca = c * _sigmoid(c)
    o_ref[0] = jnp.dot(ca, w_ref[0], preferred_element_type=F32,
                       precision=lax.Precision.HIGHEST) + b_ref[0]


def _ada_mod(c, ada_w, ada_b):
    depth, d, n = ada_w.shape
    bsz = c.shape[0]
    rows = -(-bsz // SUBLANES) * SUBLANES
    cp = jnp.zeros((rows, d), F32).at[:bsz].set(c)
    tn = 1024
    out = pl.pallas_call(
        _mod_kernel,
        grid=(depth, n // tn),
        in_specs=[
            pl.BlockSpec((rows, d), lambda l, j: (0, 0)),
            pl.BlockSpec((1, d, tn), lambda l, j: (l, 0, j)),
            pl.BlockSpec((1, 1, tn), lambda l, j: (l, 0, j)),
        ],
        out_specs=pl.BlockSpec((1, rows, tn), lambda l, j: (l, 0, j)),
        out_shape=jax.ShapeDtypeStruct((depth, rows, n), F32),
        compiler_params=_cparams(("parallel", "parallel")),
        name="ada_mod",
    )(cp, ada_w, ada_b.reshape(depth, 1, n))
    return out[:, :bsz]


def _fox_in_kernel(x_ref, g_ref, sh_ref, sc_ref, w_ref, wfl_ref, o_ref, fl_ref, *, tn):
    hb = _norm_mod(x_ref[0], g_ref[...], sh_ref[0], sc_ref[0]).astype(BF16)
    n = w_ref.shape[1]
    for j in range(0, n, tn):
        o_ref[0, :, j:j + tn] = jnp.dot(
            hb, w_ref[:, j:j + tn], preferred_element_type=F32).astype(BF16)
    fl_ref[0] = jnp.dot(hb, wfl_ref[...], preferred_element_type=F32)


def _fox_in(x, g, sh, sc, w, wfl, tm):
    bsz, s, d = x.shape
    n = w.shape[1]
    kern = functools.partial(_fox_in_kernel, tn=1024)
    return pl.pallas_call(
        kern,
        grid=(bsz, s // tm),
        in_specs=[
            pl.BlockSpec((1, tm, d), lambda b, i: (b, i, 0)),
            _const_spec((1, d)),
            pl.BlockSpec((1, 1, d), lambda b, i: (b, 0, 0)),
            pl.BlockSpec((1, 1, d), lambda b, i: (b, 0, 0)),
            _const_spec((d, n)),
            _const_spec((d, LANES)),
        ],
        out_specs=[
            pl.BlockSpec((1, tm, n), lambda b, i: (b, i, 0)),
            pl.BlockSpec((1, tm, LANES), lambda b, i: (b, i, 0)),
        ],
        out_shape=[
            jax.ShapeDtypeStruct((bsz, s, n), BF16),
            jax.ShapeDtypeStruct((bsz, s, LANES), F32),
        ],
        compiler_params=_cparams(("parallel", "parallel")),
        name="fox_in",
    )(x, g, sh, sc, w, wfl)


def _fcum_kernel(fl_ref, bf_ref, o_ref):
    z = fl_ref[0] + bf_ref[0]
    v = jnp.minimum(z, 0.0) - jnp.log(1.0 + jnp.exp(-jnp.abs(z)))
    r = v.shape[0]
    lane = lax.broadcasted_iota(jnp.int32, v.shape, 1)
    row = lax.broadcasted_iota(jnp.int32, v.shape, 0)
    d = 1
    while d < LANES:
        v = v + jnp.where(lane >= d, pltpu.roll(v, d, axis=1), 0.0)
        d *= 2
    tot = jnp.broadcast_to(v[:, LANES - 1:LANES], v.shape)
    inc = tot
    d = 1
    while d < r:
        inc = inc + jnp.where(row >= d, pltpu.roll(inc, d, axis=0), 0.0)
        d *= 2
    o_ref[0] = v + (inc - tot)


def _fcum(fl_t, bf_rows):
    bh, r, _ = fl_t.shape
    return pl.pallas_call(
        _fcum_kernel,
        grid=(bh,),
        in_specs=[
            pl.BlockSpec((1, r, LANES), lambda i: (i, 0, 0)),
            pl.BlockSpec((1, 1, LANES), lambda i: (i, 0, 0)),
        ],
        out_specs=pl.BlockSpec((1, r, LANES), lambda i: (i, 0, 0)),
        out_shape=jax.ShapeDtypeStruct((bh, r, LANES), F32),
        compiler_params=_cparams(("parallel",)),
        name="fcum",
    )(fl_t, bf_rows)


LOG2E = 1.4426950408889634
N_PIECES = 3


def _bf16_pieces(x):
    out = []
    for _ in range(N_PIECES):
        p = x.astype(BF16).astype(F32)
        out.append(p)
        x = x - p
    return out


def _head_norm(x, gain, lane):
    x2 = x * x
    lo = lane < FOX_HEAD_DIM
    ss_a = jnp.sum(jnp.where(lo, x2, 0.0), axis=-1, keepdims=True)
    ss_b = jnp.sum(jnp.where(lo, 0.0, x2), axis=-1, keepdims=True)
    ms = jnp.where(lo, ss_a, ss_b) * (1.0 / FOX_HEAD_DIM)
    return x * lax.rsqrt(ms + EPS) * gain


def _attn_kernel(q_ref, k_ref, v_ref, o_ref, f_ref, qg_ref, kg_ref, out_ref,
                 kx_ref, vt_ref, qt_ref, m_ref, acc_ref, *, tq, prep_rows):
    qi = pl.program_id(2)
    s_len = k_ref.shape[1]
    half = FOX_HEAD_DIM

    @pl.when(qi == 0)
    def _prepare_kv():
        lane = lax.broadcasted_iota(jnp.int32, (prep_rows, LANES), 1)
        rowt = lax.broadcasted_iota(jnp.int32, (LANES, prep_rows), 0)

        def body(c, carry):
            r0 = pl.multiple_of(c * prep_rows, prep_rows)
            kn = _head_norm(k_ref[0, pl.ds(r0, prep_rows), :].astype(F32), kg_ref[...], lane)
            kx_ref[pl.ds(r0, prep_rows), 0:LANES] = kn.astype(BF16)
            ext = jnp.zeros((LANES, prep_rows), F32)
            for hh in range(2):
                f2 = f_ref[0, 0, hh:hh + 1, pl.ds(r0, prep_rows)] * (-LOG2E)
                for j, piece in enumerate(_bf16_pieces(f2)):
                    ext = jnp.where(rowt == hh * N_PIECES + j, piece, ext)
            kx_ref[pl.ds(r0, prep_rows), LANES:2 * LANES] = ext.T.astype(BF16)
            vt = v_ref[0, pl.ds(r0, prep_rows), :].astype(F32).T
            vt_ref[0, :, pl.ds(r0, prep_rows)] = jnp.where(rowt < half, vt, 1.0).astype(BF16)
            vt_ref[1, :, pl.ds(r0, prep_rows)] = jnp.where(rowt < half, 1.0, vt).astype(BF16)
            return carry

        lax.fori_loop(0, s_len // prep_rows, body, 0)

    lane = lax.broadcasted_iota(jnp.int32, (tq, LANES), 1)
    qn = _head_norm(q_ref[0].astype(F32), qg_ref[...], lane) * (FOX_HEAD_DIM ** -0.5 * LOG2E)
    qt = qn.T
    rowq = lax.broadcasted_iota(jnp.int32, (LANES, tq), 0)
    qt_ref[0:LANES, 0:tq] = jnp.where(rowq < half, qt, 0.0).astype(BF16)
    qt_ref[0:LANES, tq:2 * tq] = jnp.where(rowq < half, 0.0, qt).astype(BF16)
    qt_ref[LANES:2 * LANES, 0:tq] = jnp.where(rowq < N_PIECES, 1.0, 0.0).astype(BF16)
    qt_ref[LANES:2 * LANES, tq:2 * tq] = jnp.where(
        (rowq >= N_PIECES) & (rowq < 2 * N_PIECES), 1.0, 0.0).astype(BF16)
    m_ref[...] = jnp.full(m_ref.shape, NEG_BIG, F32)
    acc_ref[...] = jnp.zeros(acc_ref.shape, F32)

    def block(c0, masked):
        st = jnp.dot(kx_ref[pl.ds(c0, tq), :], qt_ref[...], preferred_element_type=F32)
        if masked:
            key = lax.broadcasted_iota(jnp.int32, (tq, 2 * tq), 0)
            qry = lax.broadcasted_iota(jnp.int32, (tq, 2 * tq), 1)
            qry = jnp.where(qry >= tq, qry - tq, qry)
            st = jnp.where(key <= qry, st, NEG_BIG)
        m_old = m_ref[...]
        m_new = jnp.maximum(m_old, jnp.max(st, axis=0, keepdims=True))
        alpha = jnp.exp2(m_old - m_new)
        pt = jnp.exp2(st - m_new).astype(BF16)
        m_ref[...] = m_new
        for hh in range(2):
            cols = slice(hh * tq, (hh + 1) * tq)
            pv = jnp.dot(vt_ref[hh, :, pl.ds(c0, tq)], pt[:, cols], preferred_element_type=F32)
            acc_ref[:, cols] = acc_ref[:, cols] * alpha[:, cols] + pv

    def body(kb, carry):
        block(pl.multiple_of(kb * tq, tq), False)
        return carry

    lax.fori_loop(0, qi, body, 0)
    block(pl.multiple_of(qi * tq, tq), True)

    acc = acc_ref[...]
    res = jnp.concatenate([acc[0:half, 0:tq] / acc[half:, 0:tq],
                           acc[half:, tq:] / acc[0:half, tq:]], axis=0)
    out_ref[0] = (res.T * _sigmoid(o_ref[0].astype(F32))).astype(BF16)


def _attention(qkvo, f_pairs, qg2, kg2, tq):
    bsz, s, n = qkvo.shape
    d = n // 4
    npairs = d // LANES
    kern = functools.partial(_attn_kernel, tq=tq, prep_rows=min(512, s))
    return pl.pallas_call(
        kern,
        grid=(bsz, npairs, s // tq),
        in_specs=[
            pl.BlockSpec((1, tq, LANES), lambda b, h, i: (b, i, h)),
            pl.BlockSpec((1, s, LANES), lambda b, h, i: (b, 0, npairs + h)),
            pl.BlockSpec((1, s, LANES), lambda b, h, i: (b, 0, 2 * npairs + h)),
            pl.BlockSpec((1, tq, LANES), lambda b, h, i: (b, i, 3 * npairs + h)),
            pl.BlockSpec((1, 1, 2, s), lambda b, h, i: (b, h, 0, 0)),
            _const_spec((1, LANES)),
            _const_spec((1, LANES)),
        ],
        out_specs=pl.BlockSpec((1, tq, LANES), lambda b, h, i: (b, i, h)),
        out_shape=jax.ShapeDtypeStruct((bsz, s, d), BF16),
        scratch_shapes=[
            pltpu.VMEM((s, 2 * LANES), BF16),
            pltpu.VMEM((2, LANES, s), BF16),
            pltpu.VMEM((2 * LANES, 2 * tq), BF16),
            pltpu.VMEM((1, 2 * tq), F32),
            pltpu.VMEM((LANES, 2 * tq), F32),
        ],
        compiler_params=_cparams(("parallel", "parallel", "arbitrary")),
        name="fox_attn",
    )(qkvo, qkvo, qkvo, qkvo, f_pairs, qg2, kg2)


def _shift_rows(a, d, prev_tail):
    r = pltpu.roll(a, d, axis=0)
    row = lax.broadcasted_iota(jnp.int32, prev_tail.shape, 0)
    top = jnp.where(row < d, pltpu.roll(prev_tail, d, axis=0), r[:SUBLANES])
    return jnp.concatenate([top, r[SUBLANES:]], axis=0)


def _ffn_kernel(*refs, cw, fuse_attn_out, final_norm):
    refs = list(refs)
    x_ref = refs.pop(0)
    if fuse_attn_out:
        ao_ref, wo_ref, g1_ref = refs.pop(0), refs.pop(0), refs.pop(0)
    g_ref, sh_ref, sc_ref, g2_ref, wup_ref, cw_ref, cb_ref, wdn_ref = refs[:8]
    refs = refs[8:]
    if final_norm:
        fg_ref = refs.pop(0)
    o_ref, tail_ref, act_ref = refs

    si = pl.program_id(1)
    tm = x_ref.shape[1]
    dff = wdn_ref.shape[0]

    @pl.when(si == 0)
    def _zero_tail():
        tail_ref[...] = jnp.zeros_like(tail_ref)

    x = x_ref[0]
    if fuse_attn_out:
        x = x + g1_ref[0] * jnp.dot(ao_ref[0], wo_ref[...], preferred_element_type=F32)
    hb = _norm_mod(x, g_ref[...], sh_ref[0], sc_ref[0]).astype(BF16)

    def conv(c0):
        a = jnp.dot(hb, wup_ref[:, c0:c0 + cw], preferred_element_type=F32)
        prev = tail_ref[:, c0:c0 + cw]
        tail_ref[:, c0:c0 + cw] = a[tm - SUBLANES:]
        w = cw_ref[:, c0:c0 + cw]
        return (_shift_rows(a, 2, prev) * w[0:1] + _shift_rows(a, 1, prev) * w[1:2]
                + a * w[2:3] + cb_ref[:, c0:c0 + cw])

    for c0 in range(0, dff, cw):
        gate = conv(c0)
        val = conv(dff + c0)
        act_ref[:, c0:c0 + cw] = (gate * _sigmoid(gate) * val).astype(BF16)

    y = jnp.dot(act_ref[...], wdn_ref[...], preferred_element_type=F32)
    xo = x + g2_ref[0] * y
    if final_norm:
        ms = jnp.mean(xo * xo, axis=-1, keepdims=True)
        xo = xo * lax.rsqrt(ms + EPS) * fg_ref[...]
    o_ref[0] = xo


def _ffn(x, attn_out, norm_g, sh, sc, g2, w_up, conv_w, conv_b, w_down, final_g, tm):
    bsz, s, d = x.shape
    dff = w_down.shape[0]
    fuse_attn_out = attn_out is not None
    final_norm = final_g is not None
    row_spec = pl.BlockSpec((1, tm, d), lambda b, i: (b, i, 0))
    vec_spec = pl.BlockSpec((1, 1, d), lambda b, i: (b, 0, 0))
    args, specs = [x], [row_spec]
    if fuse_attn_out:
        ao, w_out, g1 = attn_out
        args += [ao, w_out, g1]
        specs += [row_spec, _const_spec((d, d)), vec_spec]
    args += [norm_g, sh, sc, g2, w_up, conv_w, conv_b, w_down]
    specs += [_const_spec((1, d)), vec_spec, vec_spec, vec_spec,
              _const_spec((d, 2 * dff)), _const_spec((CONV_WIDTH, 2 * dff)),
              _const_spec((1, 2 * dff)), _const_spec((dff, d))]
    if final_norm:
        args.append(final_g)
        specs.append(_const_spec((1, d)))
    kern = functools.partial(_ffn_kernel, cw=256, fuse_attn_out=fuse_attn_out,
                             final_norm=final_norm)
    return pl.pallas_call(
        kern,
        grid=(bsz, s // tm),
        in_specs=specs,
        out_specs=row_spec,
        out_shape=jax.ShapeDtypeStruct((bsz, s, d), F32),
        scratch_shapes=[
            pltpu.VMEM((SUBLANES, 2 * dff), F32),
            pltpu.VMEM((tm, dff), BF16),
        ],
        compiler_params=_cparams(("parallel", "arbitrary")),
        name="conv_ffn",
    )(*args)


def _gelu_tanh(x):
    return 0.5 * x * (1.0 + jnp.tanh(0.7978845608028654 * (x + 0.044715 * (x * x * x))))


def _sgu_kernel(x_ref, g_ref, sh_ref, sc_ref, g1_ref, win_ref, bin_ref, vg_ref, vb_ref,
                ws_ref, bs_ref, wout_ref, o_ref, v_ref, vn_ref, y_ref):
    tm = x_ref.shape[1]
    width = vg_ref.shape[1]
    gdim = width // SGU_GROUPS
    x = x_ref[0]
    hb = _norm_mod(x, g_ref[...], sh_ref[0], sc_ref[0]).astype(BF16)

    def z_cols(c0, n):
        return _gelu_tanh(jnp.dot(hb, win_ref[:, c0:c0 + n], preferred_element_type=F32)
                          + bin_ref[:, c0:c0 + n])

    vcw = 512
    for c0 in range(0, width, vcw):
        v_ref[:, c0:c0 + vcw] = z_cols(width + c0, vcw)
    v = v_ref[...]
    mu = jnp.mean(v, axis=-1, keepdims=True)
    dv = v - mu
    var = jnp.mean(dv * dv, axis=-1, keepdims=True)
    vn_ref[...] = (dv * lax.rsqrt(var + EPS) * vg_ref[...] + vb_ref[...]).astype(BF16)

    t = lax.broadcasted_iota(jnp.int32, (SGU_BLOCK, SGU_BLOCK), 0)
    s = lax.broadcasted_iota(jnp.int32, (SGU_BLOCK, SGU_BLOCK), 1)
    allowed = (s // CHUNK) <= (t // CHUNK)
    for g in range(SGU_GROUPS):
        c0 = g * gdim
        u = z_cols(c0, gdim)
        ws = jnp.where(allowed, ws_ref[g], 0.0).astype(BF16)
        bias = jnp.concatenate([bs_ref[g]] * (gdim // LANES), axis=1)
        for r0 in range(0, tm, SGU_BLOCK):
            mixed = jnp.dot(ws, vn_ref[r0:r0 + SGU_BLOCK, c0:c0 + gdim],
                            preferred_element_type=F32) + bias
            y_ref[r0:r0 + SGU_BLOCK, c0:c0 + gdim] = (u[r0:r0 + SGU_BLOCK] * mixed).astype(BF16)

    y = jnp.dot(y_ref[...], wout_ref[...], preferred_element_type=F32)
    o_ref[0] = x + g1_ref[0] * y


def _sgu(x, norm_g, sh, sc, g1, w_in, b_in, v_gain, v_bias, w_s, bs_b, w_out, tm):
    bsz, s, d = x.shape
    width = w_out.shape[0]
    row_spec = pl.BlockSpec((1, tm, d), lambda b, i: (b, i, 0))
    vec_spec = pl.BlockSpec((1, 1, d), lambda b, i: (b, 0, 0))
    return pl.pallas_call(
        _sgu_kernel,
        grid=(bsz, s // tm),
        in_specs=[
            row_spec, _const_spec((1, d)), vec_spec, vec_spec, vec_spec,
            _const_spec((d, 2 * width)), _const_spec((1, 2 * width)),
            _const_spec((1, width)), _const_spec((1, width)),
            _const_spec((SGU_GROUPS, SGU_BLOCK, SGU_BLOCK)),
            _const_spec((SGU_GROUPS, SGU_BLOCK, LANES)),
            _const_spec((width, d)),
        ],
        out_specs=row_spec,
        out_shape=jax.ShapeDtypeStruct((bsz, s, d), F32),
        scratch_shapes=[
            pltpu.VMEM((tm, width), F32),
            pltpu.VMEM((tm, width), BF16),
            pltpu.VMEM((tm, width), BF16),
        ],
        compiler_params=_cparams(("parallel", "parallel")),
        name="sgu",
    )(x, norm_g, sh, sc, g1, w_in, b_in, v_gain, v_bias, w_s, bs_b, w_out)


def kernel(x, c, fox_w_in, fox_b_f, fox_q_gain, fox_k_gain, fox_w_out, sgu_w_in, sgu_b_in,
           sgu_v_gain, sgu_v_bias, sgu_w_s, sgu_b_s, sgu_w_out, ffn_w_up, ffn_conv_w,
           ffn_conv_b, ffn_w_down, ada_w, ada_b, norm1_g, norm2_g, final_g):
    bsz, s, d = x.shape
    depth = ada_w.shape[0]
    assert depth == 2 and d == FOX_HEADS * FOX_HEAD_DIM
    assert s % 512 == 0
    tm = 512
    tq = 512

    mod = _ada_mod(c, ada_w, ada_b)
    mods = [[mod[l, :, k * d:(k + 1) * d].reshape(bsz, 1, d) for k in range(6)]
            for l in range(depth)]

    sh1, sc1, g1, sh2, sc2, g2 = mods[0]
    w_in = fox_w_in[0]
    w_main = w_in[:, :4 * d].astype(BF16)
    w_fl = jnp.zeros((d, LANES), BF16).at[:, :FOX_HEADS].set(w_in[:, 4 * d:].astype(BF16))
    qkvo, fl = _fox_in(x, norm1_g[0].reshape(1, d), sh1, sc1, w_main, w_fl, tm)

    fl_t = jnp.transpose(fl[:, :, :FOX_HEADS], (0, 2, 1)).reshape(
        bsz * FOX_HEADS, s // LANES, LANES)
    bf_rows = jnp.broadcast_to(jnp.tile(fox_b_f[0], bsz)[:, None, None],
                               (bsz * FOX_HEADS, 1, LANES))
    f_pairs = _fcum(fl_t, bf_rows).reshape(bsz, FOX_HEADS // 2, 2, s)

    qg2 = jnp.tile(fox_q_gain[0], 2).reshape(1, LANES)
    kg2 = jnp.tile(fox_k_gain[0], 2).reshape(1, LANES)
    ao = _attention(qkvo, f_pairs, qg2, kg2, tq)

    x = _ffn(x, (ao, fox_w_out[0].astype(BF16), g1), norm2_g[0].reshape(1, d), sh2, sc2, g2,
             ffn_w_up[0].astype(BF16), ffn_conv_w[0], ffn_conv_b[0].reshape(1, -1),
             ffn_w_down[0].astype(BF16), None, tm)

    sh1, sc1, g1, sh2, sc2, g2 = mods[1]
    width = sgu_w_out.shape[1]
    bs_b = jnp.broadcast_to(sgu_b_s[0][:, :, None], (SGU_GROUPS, SGU_BLOCK, LANES))
    x = _sgu(x, norm1_g[1].reshape(1, d), sh1, sc1, g1, sgu_w_in[0].astype(BF16),
             sgu_b_in[0].reshape(1, -1), sgu_v_gain[0].reshape(1, width),
             sgu_v_bias[0].reshape(1, width), sgu_w_s[0], bs_b, sgu_w_out[0].astype(BF16), tm)
    x = _ffn(x, None, norm2_g[1].reshape(1, d), sh2, sc2, g2,
             ffn_w_up[1].astype(BF16), ffn_conv_w[1], ffn_conv_b[1].reshape(1, -1),
             ffn_w_down[1].astype(BF16), final_g.reshape(1, d), tm)
    return x
```

```python
import functools

import jax
import jax.numpy as jnp
from jax import lax
from jax.experimental import pallas as pl
from jax.experimental.pallas import tpu as pltpu

F32 = jnp.float32
BF16 = jnp.bfloat16

FOX_HEADS = 16
FOX_HEAD_DIM = 64
SGU_GROUPS = 8
SGU_BLOCK = 128
CHUNK = 64
CONV_WIDTH = 3
EPS = 1e-6

LANES = 128
SUBLANES = 8
VMEM_LIMIT = 56 * 1024 * 1024
NEG_BIG = -1e30


def _cparams(sem):
    return pltpu.CompilerParams(dimension_semantics=sem, vmem_limit_bytes=VMEM_LIMIT)


def _const_spec(shape):
    nd = len(shape)
    return pl.BlockSpec(shape, lambda *_: (0,) * nd, pipeline_mode=pl.Buffered(1))


def _sigmoid(x):
    return 1.0 / (1.0 + jnp.exp(-x))


def _norm_mod(x, g, sh, sc):
    ms = jnp.mean(x * x, axis=-1, keepdims=True)
    h = x * lax.rsqrt(ms + EPS) * g
    return h * (1.0 + sc) + sh


def _mod_kernel(c_ref, w_ref, b_ref, o_ref):
    c = c_ref[...]
    ca = c * _sigmoid(c)
    o_ref[0] = jnp.dot(ca, w_ref[0], preferred_element_type=F32,
                       precision=lax.Precision.HIGHEST) + b_ref[0]


def _ada_mod(c, ada_w, ada_b):
    depth, d, n = ada_w.shape
    bsz = c.shape[0]
    rows = -(-bsz // SUBLANES) * SUBLANES
    cp = jnp.zeros((rows, d), F32).at[:bsz].set(c)
    tn = 1024
    out = pl.pallas_call(
        _mod_kernel,
        grid=(depth, n // tn),
        in_specs=[
            pl.BlockSpec((rows, d), lambda l, j: (0, 0)),
            pl.BlockSpec((1, d, tn), lambda l, j: (l, 0, j)),
            pl.BlockSpec((1, 1, tn), lambda l, j: (l, 0, j)),
        ],
        out_specs=pl.BlockSpec((1, rows, tn), lambda l, j: (l, 0, j)),
        out_shape=jax.ShapeDtypeStruct((depth, rows, n), F32),
        compiler_params=_cparams(("parallel", "parallel")),
        name="ada_mod",
    )(cp, ada_w, ada_b.reshape(depth, 1, n))
    return out[:, :bsz]


def _fox_in_kernel(x_ref, g_ref, sh_ref, sc_ref, w_ref, wfl_ref, o_ref, fl_ref, *, tn):
    hb = _norm_mod(x_ref[0], g_ref[...], sh_ref[0], sc_ref[0]).astype(BF16)
    n = w_ref.shape[1]
    for j in range(0, n, tn):
        o_ref[0, :, j:j + tn] = jnp.dot(
            hb, w_ref[:, j:j + tn], preferred_element_type=F32).astype(BF16)
    fl_ref[0] = jnp.dot(hb, wfl_ref[...], preferred_element_type=F32)


def _fox_in(x, g, sh, sc, w, wfl, tm):
    bsz, s, d = x.shape
    n = w.shape[1]
    kern = functools.partial(_fox_in_kernel, tn=1024)
    return pl.pallas_call(
        kern,
        grid=(bsz, s // tm),
        in_specs=[
            pl.BlockSpec((1, tm, d), lambda b, i: (b, i, 0)),
            _const_spec((1, d)),
            pl.BlockSpec((1, 1, d), lambda b, i: (b, 0, 0)),
            pl.BlockSpec((1, 1, d), lambda b, i: (b, 0, 0)),
            _const_spec((d, n)),
            _const_spec((d, LANES)),
        ],
        out_specs=[
            pl.BlockSpec((1, tm, n), lambda b, i: (b, i, 0)),
            pl.BlockSpec((1, tm, LANES), lambda b, i: (b, i, 0)),
        ],
        out_shape=[
            jax.ShapeDtypeStruct((bsz, s, n), BF16),
            jax.ShapeDtypeStruct((bsz, s, LANES), F32),
        ],
        compiler_params=_cparams(("parallel", "parallel")),
        name="fox_in",
    )(x, g, sh, sc, w, wfl)


def _fcum_kernel(fl_ref, bf_ref, o_ref):
    z = fl_ref[0] + bf_ref[0]
    v = jnp.minimum(z, 0.0) - jnp.log(1.0 + jnp.exp(-jnp.abs(z)))
    r = v.shape[0]
    lane = lax.broadcasted_iota(jnp.int32, v.shape, 1)
    row = lax.broadcasted_iota(jnp.int32, v.shape, 0)
    d = 1
    while d < LANES:
        v = v + jnp.where(lane >= d, pltpu.roll(v, d, axis=1), 0.0)
        d *= 2
    tot = jnp.broadcast_to(v[:, LANES - 1:LANES], v.shape)
    inc = tot
    d = 1
    while d < r:
        inc = inc + jnp.where(row >= d, pltpu.roll(inc, d, axis=0), 0.0)
        d *= 2
    o_ref[0] = v + (inc - tot)


def _fcum(fl_t, bf_rows):
    bh, r, _ = fl_t.shape
    return pl.pallas_call(
        _fcum_kernel,
        grid=(bh,),
        in_specs=[
            pl.BlockSpec((1, r, LANES), lambda i: (i, 0, 0)),
            pl.BlockSpec((1, 1, LANES), lambda i: (i, 0, 0)),
        ],
        out_specs=pl.BlockSpec((1, r, LANES), lambda i: (i, 0, 0)),
        out_shape=jax.ShapeDtypeStruct((bh, r, LANES), F32),
        compiler_params=_cparams(("parallel",)),
        name="fcum",
    )(fl_t, bf_rows)


LOG2E = 1.4426950408889634
N_PIECES = 3


def _bf16_pieces(x):
    out = []
    for _ in range(N_PIECES):
        p = x.astype(BF16).astype(F32)
        out.append(p)
        x = x - p
    return out


def _head_norm(x, gain, lane):
    x2 = x * x
    lo = lane < FOX_HEAD_DIM
    ss_a = jnp.sum(jnp.where(lo, x2, 0.0), axis=-1, keepdims=True)
    ss_b = jnp.sum(jnp.where(lo, 0.0, x2), axis=-1, keepdims=True)
    ms = jnp.where(lo, ss_a, ss_b) * (1.0 / FOX_HEAD_DIM)
    return x * lax.rsqrt(ms + EPS) * gain


def _attn_kernel(q_ref, k_ref, v_ref, o_ref, f_ref, qg_ref, kg_ref, out_ref,
                 kx_ref, vt_ref, qt_ref, m_ref, acc_ref, sta_ref, stb_ref, *, tq, prep_rows):
    qi = pl.program_id(2)
    s_len = k_ref.shape[1]
    half = FOX_HEAD_DIM

    @pl.when(qi == 0)
    def _prepare_kv():
        lane = lax.broadcasted_iota(jnp.int32, (prep_rows, LANES), 1)
        rowt = lax.broadcasted_iota(jnp.int32, (LANES, prep_rows), 0)

        def body(c, carry):
            r0 = pl.multiple_of(c * prep_rows, prep_rows)
            kn = _head_norm(k_ref[0, pl.ds(r0, prep_rows), :].astype(F32), kg_ref[...], lane)
            kx_ref[pl.ds(r0, prep_rows), 0:LANES] = kn.astype(BF16)
            ext = jnp.zeros((LANES, prep_rows), F32)
            for hh in range(2):
                f2 = f_ref[0, 0, hh:hh + 1, pl.ds(r0, prep_rows)] * (-LOG2E)
                for j, piece in enumerate(_bf16_pieces(f2)):
                    ext = jnp.where(rowt == hh * N_PIECES + j, piece, ext)
            kx_ref[pl.ds(r0, prep_rows), LANES:2 * LANES] = ext.T.astype(BF16)
            vt = v_ref[0, pl.ds(r0, prep_rows), :].astype(F32).T
            vt_ref[0, :, pl.ds(r0, prep_rows)] = jnp.where(rowt < half, vt, 1.0).astype(BF16)
            vt_ref[1, :, pl.ds(r0, prep_rows)] = jnp.where(rowt < half, 1.0, vt).astype(BF16)
            return carry

        lax.fori_loop(0, s_len // prep_rows, body, 0)

    lane = lax.broadcasted_iota(jnp.int32, (tq, LANES), 1)
    qn = _head_norm(q_ref[0].astype(F32), qg_ref[...], lane) * (FOX_HEAD_DIM ** -0.5 * LOG2E)
    qt = qn.T
    rowq = lax.broadcasted_iota(jnp.int32, (LANES, tq), 0)
    qt_ref[0:LANES, 0:tq] = jnp.where(rowq < half, qt, 0.0).astype(BF16)
    qt_ref[0:LANES, tq:2 * tq] = jnp.where(rowq < half, 0.0, qt).astype(BF16)
    qt_ref[LANES:2 * LANES, 0:tq] = jnp.where(rowq < N_PIECES, 1.0, 0.0).astype(BF16)
    qt_ref[LANES:2 * LANES, tq:2 * tq] = jnp.where(
        (rowq >= N_PIECES) & (rowq < 2 * N_PIECES), 1.0, 0.0).astype(BF16)
    m_ref[...] = jnp.full(m_ref.shape, NEG_BIG, F32)
    acc_ref[...] = jnp.zeros(acc_ref.shape, F32)

    def logits(c0, hh):
        return jnp.dot(kx_ref[pl.ds(c0, tq), :], qt_ref[:, hh * tq:(hh + 1) * tq],
                       preferred_element_type=F32)

    def accumulate(st_ref, c0, hh, masked):
        st = st_ref[...]
        if masked:
            key = lax.broadcasted_iota(jnp.int32, (tq, tq), 0)
            qry = lax.broadcasted_iota(jnp.int32, (tq, tq), 1)
            st = jnp.where(key <= qry, st, NEG_BIG)
        cols = slice(hh * tq, (hh + 1) * tq)
        m_old = m_ref[:, cols]
        m_new = jnp.maximum(m_old, jnp.max(st, axis=0, keepdims=True))
        alpha = jnp.exp2(m_old - m_new)
        pt = jnp.exp2(st - m_new).astype(BF16)
        m_ref[:, cols] = m_new
        pv = jnp.dot(vt_ref[hh, :, pl.ds(c0, tq)], pt, preferred_element_type=F32)
        acc_ref[:, cols] = acc_ref[:, cols] * alpha + pv

    sta_ref[...] = logits(0, 0)

    def body(kb, carry):
        c0 = pl.multiple_of(kb * tq, tq)
        stb_ref[...] = logits(c0, 1)
        accumulate(sta_ref, c0, 0, False)
        sta_ref[...] = logits(c0 + tq, 0)
        accumulate(stb_ref, c0, 1, False)
        return carry

    def body_pair(j, carry):
        return body(2 * j + 1, body(2 * j, carry))

    n_pairs = lax.shift_right_logical(qi, 1)
    lax.fori_loop(0, n_pairs, body_pair, 0)
    lax.fori_loop(2 * n_pairs, qi, body, 0)
    cd = pl.multiple_of(qi * tq, tq)
    stb_ref[...] = logits(cd, 1)
    accumulate(sta_ref, cd, 0, True)
    accumulate(stb_ref, cd, 1, True)

    acc = acc_ref[...]
    res = jnp.concatenate([acc[0:half, 0:tq] / acc[half:, 0:tq],
                           acc[half:, tq:] / acc[0:half, tq:]], axis=0)
    out_ref[0] = (res.T * _sigmoid(o_ref[0].astype(F32))).astype(BF16)


def _attention(qkvo, f_pairs, qg2, kg2, tq):
    bsz, s, n = qkvo.shape
    d = n // 4
    npairs = d // LANES
    kern = functools.partial(_attn_kernel, tq=tq, prep_rows=min(512, s))
    return pl.pallas_call(
        kern,
        grid=(bsz, npairs, s // tq),
        in_specs=[
            pl.BlockSpec((1, tq, LANES), lambda b, h, i: (b, i, h)),
            pl.BlockSpec((1, s, LANES), lambda b, h, i: (b, 0, npairs + h)),
            pl.BlockSpec((1, s, LANES), lambda b, h, i: (b, 0, 2 * npairs + h)),
            pl.BlockSpec((1, tq, LANES), lambda b, h, i: (b, i, 3 * npairs + h)),
            pl.BlockSpec((1, 1, 2, s), lambda b, h, i: (b, h, 0, 0)),
            _const_spec((1, LANES)),
            _const_spec((1, LANES)),
        ],
        out_specs=pl.BlockSpec((1, tq, LANES), lambda b, h, i: (b, i, h)),
        out_shape=jax.ShapeDtypeStruct((bsz, s, d), BF16),
        scratch_shapes=[
            pltpu.VMEM((s, 2 * LANES), BF16),
            pltpu.VMEM((2, LANES, s), BF16),
            pltpu.VMEM((2 * LANES, 2 * tq), BF16),
            pltpu.VMEM((1, 2 * tq), F32),
            pltpu.VMEM((LANES, 2 * tq), F32),
            pltpu.VMEM((tq, tq), F32),
            pltpu.VMEM((tq, tq), F32),
        ],
        compiler_params=_cparams(("parallel", "parallel", "arbitrary")),
        name="fox_attn",
    )(qkvo, qkvo, qkvo, qkvo, f_pairs, qg2, kg2)


def _shift_rows(a, d, prev_tail):
    r = pltpu.roll(a, d, axis=0)
    row = lax.broadcasted_iota(jnp.int32, prev_tail.shape, 0)
    top = jnp.where(row < d, pltpu.roll(prev_tail, d, axis=0), r[:SUBLANES])
    return jnp.concatenate([top, r[SUBLANES:]], axis=0)


def _ffn_kernel(*refs, cw, fuse_attn_out, final_norm):
    refs = list(refs)
    x_ref = refs.pop(0)
    if fuse_attn_out:
        ao_ref, wo_ref, g1_ref = refs.pop(0), refs.pop(0), refs.pop(0)
    g_ref, sh_ref, sc_ref, g2_ref, wup_ref, cw_ref, cb_ref, wdn_ref = refs[:8]
    refs = refs[8:]
    if final_norm:
        fg_ref = refs.pop(0)
    o_ref, tail_ref, act_ref = refs

    si = pl.program_id(1)
    tm = x_ref.shape[1]
    dff = wdn_ref.shape[0]

    @pl.when(si == 0)
    def _zero_tail():
        tail_ref[...] = jnp.zeros_like(tail_ref)

    x = x_ref[0]
    if fuse_attn_out:
        x = x + g1_ref[0] * jnp.dot(ao_ref[0], wo_ref[...], preferred_element_type=F32)
    hb = _norm_mod(x, g_ref[...], sh_ref[0], sc_ref[0]).astype(BF16)

    def conv(c0):
        a = jnp.dot(hb, wup_ref[:, c0:c0 + cw], preferred_element_type=F32)
        prev = tail_ref[:, c0:c0 + cw]
        tail_ref[:, c0:c0 + cw] = a[tm - SUBLANES:]
        w = cw_ref[:, c0:c0 + cw]
        return (_shift_rows(a, 2, prev) * w[0:1] + _shift_rows(a, 1, prev) * w[1:2]
                + a * w[2:3] + cb_ref[:, c0:c0 + cw])

    for c0 in range(0, dff, cw):
        gate = conv(c0)
        val = conv(dff + c0)
        act_ref[:, c0:c0 + cw] = (gate * _sigmoid(gate) * val).astype(BF16)

    y = jnp.dot(act_ref[...], wdn_ref[...], preferred_element_type=F32)
    xo = x + g2_ref[0] * y
    if final_norm:
        ms = jnp.mean(xo * xo, axis=-1, keepdims=True)
        xo = xo * lax.rsqrt(ms + EPS) * fg_ref[...]
    o_ref[0] = xo


def _ffn(x, attn_out, norm_g, sh, sc, g2, w_up, conv_w, conv_b, w_down, final_g, tm):
    bsz, s, d = x.shape
    dff = w_down.shape[0]
    fuse_attn_out = attn_out is not None
    final_norm = final_g is not None
    row_spec = pl.BlockSpec((1, tm, d), lambda b, i: (b, i, 0))
    vec_spec = pl.BlockSpec((1, 1, d), lambda b, i: (b, 0, 0))
    args, specs = [x], [row_spec]
    if fuse_attn_out:
        ao, w_out, g1 = attn_out
        args += [ao, w_out, g1]
        specs += [row_spec, _const_spec((d, d)), vec_spec]
    args += [norm_g, sh, sc, g2, w_up, conv_w, conv_b, w_down]
    specs += [_const_spec((1, d)), vec_spec, vec_spec, vec_spec,
              _const_spec((d, 2 * dff)), _const_spec((CONV_WIDTH, 2 * dff)),
              _const_spec((1, 2 * dff)), _const_spec((dff, d))]
    if final_norm:
        args.append(final_g)
        specs.append(_const_spec((1, d)))
    kern = functools.partial(_ffn_kernel, cw=256, fuse_attn_out=fuse_attn_out,
                             final_norm=final_norm)
    return pl.pallas_call(
        kern,
        grid=(bsz, s // tm),
        in_specs=specs,
        out_specs=row_spec,
        out_shape=jax.ShapeDtypeStruct((bsz, s, d), F32),
        scratch_shapes=[
            pltpu.VMEM((SUBLANES, 2 * dff), F32),
            pltpu.VMEM((tm, dff), BF16),
        ],
        compiler_params=_cparams(("parallel", "arbitrary")),
        name="conv_ffn",
    )(*args)


def _gelu_tanh(x):
    return 0.5 * x * (1.0 + jnp.tanh(0.7978845608028654 * (x + 0.044715 * (x * x * x))))


def _sgu_kernel(x_ref, g_ref, sh_ref, sc_ref, g1_ref, win_ref, bin_ref, vg_ref, vb_ref,
                ws_ref, bs_ref, wout_ref, o_ref, v_ref, vn_ref, y_ref):
    tm = x_ref.shape[1]
    width = vg_ref.shape[1]
    gdim = width // SGU_GROUPS
    x = x_ref[0]
    hb = _norm_mod(x, g_ref[...], sh_ref[0], sc_ref[0]).astype(BF16)

    def z_cols(c0, n):
        return _gelu_tanh(jnp.dot(hb, win_ref[:, c0:c0 + n], preferred_element_type=F32)
                          + bin_ref[:, c0:c0 + n])

    vcw = 512
    for c0 in range(0, width, vcw):
        v_ref[:, c0:c0 + vcw] = z_cols(width + c0, vcw)
    v = v_ref[...]
    mu = jnp.mean(v, axis=-1, keepdims=True)
    dv = v - mu
    var = jnp.mean(dv * dv, axis=-1, keepdims=True)
    vn_ref[...] = (dv * lax.rsqrt(var + EPS) * vg_ref[...] + vb_ref[...]).astype(BF16)

    t = lax.broadcasted_iota(jnp.int32, (SGU_BLOCK, SGU_BLOCK), 0)
    s = lax.broadcasted_iota(jnp.int32, (SGU_BLOCK, SGU_BLOCK), 1)
    allowed = (s // CHUNK) <= (t // CHUNK)
    for g in range(SGU_GROUPS):
        c0 = g * gdim
        u = z_cols(c0, gdim)
        ws = jnp.where(allowed, ws_ref[g], 0.0).astype(BF16)
        bias = jnp.concatenate([bs_ref[g]] * (gdim // LANES), axis=1)
        for r0 in range(0, tm, SGU_BLOCK):
            mixed = jnp.dot(ws, vn_ref[r0:r0 + SGU_BLOCK, c0:c0 + gdim],
                            preferred_element_type=F32) + bias
            y_ref[r0:r0 + SGU_BLOCK, c0:c0 + gdim] = (u[r0:r0 + SGU_BLOCK] * mixed).astype(BF16)

    y = jnp.dot(y_ref[...], wout_ref[...], preferred_element_type=F32)
    o_ref[0] = x + g1_ref[0] * y


def _sgu(x, norm_g, sh, sc, g1, w_in, b_in, v_gain, v_bias, w_s, bs_b, w_out, tm):
    bsz, s, d = x.shape
    width = w_out.shape[0]
    row_spec = pl.BlockSpec((1, tm, d), lambda b, i: (b, i, 0))
    vec_spec = pl.BlockSpec((1, 1, d), lambda b, i: (b, 0, 0))
    return pl.pallas_call(
        _sgu_kernel,
        grid=(bsz, s // tm),
        in_specs=[
            row_spec, _const_spec((1, d)), vec_spec, vec_spec, vec_spec,
            _const_spec((d, 2 * width)), _const_spec((1, 2 * width)),
            _const_spec((1, width)), _const_spec((1, width)),
            _const_spec((SGU_GROUPS, SGU_BLOCK, SGU_BLOCK)),
            _const_spec((SGU_GROUPS, SGU_BLOCK, LANES)),
            _const_spec((width, d)),
        ],
        out_specs=row_spec,
        out_shape=jax.ShapeDtypeStruct((bsz, s, d), F32),
        scratch_shapes=[
            pltpu.VMEM((tm, width), F32),
            pltpu.VMEM((tm, width), BF16),
            pltpu.VMEM((tm, width), BF16),
        ],
        compiler_params=_cparams(("parallel", "parallel")),
        name="sgu",
    )(x, norm_g, sh, sc, g1, w_in, b_in, v_gain, v_bias, w_s, bs_b, w_out)


def kernel(x, c, fox_w_in, fox_b_f, fox_q_gain, fox_k_gain, fox_w_out, sgu_w_in, sgu_b_in,
           sgu_v_gain, sgu_v_bias, sgu_w_s, sgu_b_s, sgu_w_out, ffn_w_up, ffn_conv_w,
           ffn_conv_b, ffn_w_down, ada_w, ada_b, norm1_g, norm2_g, final_g):
    bsz, s, d = x.shape
    depth = ada_w.shape[0]
    assert depth == 2 and d == FOX_HEADS * FOX_HEAD_DIM
    assert s % 512 == 0
    tm = 512
    tq = 512

    mod = _ada_mod(c, ada_w, ada_b)
    mods = [[mod[l, :, k * d:(k + 1) * d].reshape(bsz, 1, d) for k in range(6)]
            for l in range(depth)]

    sh1, sc1, g1, sh2, sc2, g2 = mods[0]
    w_in = fox_w_in[0]
    w_main = w_in[:, :4 * d].astype(BF16)
    w_fl = jnp.zeros((d, LANES), BF16).at[:, :FOX_HEADS].set(w_in[:, 4 * d:].astype(BF16))
    qkvo, fl = _fox_in(x, norm1_g[0].reshape(1, d), sh1, sc1, w_main, w_fl, tm)

    fl_t = jnp.transpose(fl[:, :, :FOX_HEADS], (0, 2, 1)).reshape(
        bsz * FOX_HEADS, s // LANES, LANES)
    bf_rows = jnp.broadcast_to(jnp.tile(fox_b_f[0], bsz)[:, None, None],
                               (bsz * FOX_HEADS, 1, LANES))
    f_pairs = _fcum(fl_t, bf_rows).reshape(bsz, FOX_HEADS // 2, 2, s)

    qg2 = jnp.tile(fox_q_gain[0], 2).reshape(1, LANES)
    kg2 = jnp.tile(fox_k_gain[0], 2).reshape(1, LANES)
    ao = _attention(qkvo, f_pairs, qg2, kg2, tq)

    x = _ffn(x, (ao, fox_w_out[0].astype(BF16), g1), norm2_g[0].reshape(1, d), sh2, sc2, g2,
             ffn_w_up[0].astype(BF16), ffn_conv_w[0], ffn_conv_b[0].reshape(1, -1),
             ffn_w_down[0].astype(BF16), None, tm)

    sh1, sc1, g1, sh2, sc2, g2 = mods[1]
    width = sgu_w_out.shape[1]
    bs_b = jnp.broadcast_to(sgu_b_s[0][:, :, None], (SGU_GROUPS, SGU_BLOCK, LANES))
    x = _sgu(x, norm1_g[1].reshape(1, d), sh1, sc1, g1, sgu_w_in[0].astype(BF16),
             sgu_b_in[0].reshape(1, -1), sgu_v_gain[0].reshape(1, width),
             sgu_v_bias[0].reshape(1, width), sgu_w_s[0], bs_b, sgu_w_out[0].astype(BF16), tm)
    x = _ffn(x, None, norm2_g[1].reshape(1, d), sh2, sc2, g2,
             ffn_w_up[1].astype(BF16), ffn_conv_w[1], ffn_conv_b[1].reshape(1, -1),
             ffn_w_down[1].astype(BF16), final_g.reshape(1, d), tm)
    return x
```

```python
import functools

import jax
import jax.numpy as jnp
from jax import lax
from jax.experimental import pallas as pl
from jax.experimental.pallas import tpu as pltpu

F32 = jnp.float32
BF16 = jnp.bfloat16

FOX_HEADS = 16
FOX_HEAD_DIM = 64
SGU_GROUPS = 8
SGU_BLOCK = 128
CHUNK = 64
CONV_WIDTH = 3
EPS = 1e-6

LANES = 128
SUBLANES = 8
VMEM_LIMIT = 56 * 1024 * 1024
NEG_BIG = -1e30


def _cparams(sem):
    return pltpu.CompilerParams(dimension_semantics=sem, vmem_limit_bytes=VMEM_LIMIT)


def _const_spec(shape):
    nd = len(shape)
    return pl.BlockSpec(shape, lambda *_: (0,) * nd, pipeline_mode=pl.Buffered(1))


def _sigmoid(x):
    return 1.0 / (1.0 + jnp.exp(-x))


def _norm_mod(x, g, sh, sc):
    ms = jnp.mean(x * x, axis=-1, keepdims=True)
    h = x * lax.rsqrt(ms + EPS) * g
    return h * (1.0 + sc) + sh


def _mod_kernel(c_ref, w_ref, b_ref, o_ref):
    c = c_ref[...]
    ca = c * _sigmoid(c)
    o_ref[0] = jnp.dot(ca, w_ref[0], preferred_element_type=F32,
                       precision=lax.Precision.HIGHEST) + b_ref[0]


def _ada_mod(c, ada_w, ada_b):
    depth, d, n = ada_w.shape
    bsz = c.shape[0]
    rows = -(-bsz // SUBLANES) * SUBLANES
    cp = jnp.zeros((rows, d), F32).at[:bsz].set(c)
    tn = 1024
    out = pl.pallas_call(
        _mod_kernel,
        grid=(depth, n // tn),
        in_specs=[
            pl.BlockSpec((rows, d), lambda l, j: (0, 0)),
            pl.BlockSpec((1, d, tn), lambda l, j: (l, 0, j)),
            pl.BlockSpec((1, 1, tn), lambda l, j: (l, 0, j)),
        ],
        out_specs=pl.BlockSpec((1, rows, tn), lambda l, j: (l, 0, j)),
        out_shape=jax.ShapeDtypeStruct((depth, rows, n), F32),
        compiler_params=_cparams(("parallel", "parallel")),
        name="ada_mod",
    )(cp, ada_w, ada_b.reshape(depth, 1, n))
    return out[:, :bsz]


def _fox_in_kernel(x_ref, g_ref, sh_ref, sc_ref, w_ref, wfl_ref, o_ref, fl_ref, *, tn):
    hb = _norm_mod(x_ref[0], g_ref[...], sh_ref[0], sc_ref[0]).astype(BF16)
    n = w_ref.shape[1]
    for j in range(0, n, tn):
        o_ref[0, :, j:j + tn] = jnp.dot(
            hb, w_ref[:, j:j + tn], preferred_element_type=F32).astype(BF16)
    fl_ref[0] = jnp.dot(hb, wfl_ref[...], preferred_element_type=F32)


def _fox_in(x, g, sh, sc, w, wfl, tm):
    bsz, s, d = x.shape
    n = w.shape[1]
    kern = functools.partial(_fox_in_kernel, tn=1024)
    return pl.pallas_call(
        kern,
        grid=(bsz, s // tm),
        in_specs=[
            pl.BlockSpec((1, tm, d), lambda b, i: (b, i, 0)),
            _const_spec((1, d)),
            pl.BlockSpec((1, 1, d), lambda b, i: (b, 0, 0)),
            pl.BlockSpec((1, 1, d), lambda b, i: (b, 0, 0)),
            _const_spec((d, n)),
            _const_spec((d, LANES)),
        ],
        out_specs=[
            pl.BlockSpec((1, tm, n), lambda b, i: (b, i, 0)),
            pl.BlockSpec((1, tm, LANES), lambda b, i: (b, i, 0)),
        ],
        out_shape=[
            jax.ShapeDtypeStruct((bsz, s, n), BF16),
            jax.ShapeDtypeStruct((bsz, s, LANES), F32),
        ],
        compiler_params=_cparams(("parallel", "parallel")),
        name="fox_in",
    )(x, g, sh, sc, w, wfl)


def _fcum_kernel(fl_ref, bf_ref, o_ref):
    z = fl_ref[0] + bf_ref[0]
    v = jnp.minimum(z, 0.0) - jnp.log(1.0 + jnp.exp(-jnp.abs(z)))
    r = v.shape[0]
    lane = lax.broadcasted_iota(jnp.int32, v.shape, 1)
    row = lax.broadcasted_iota(jnp.int32, v.shape, 0)
    d = 1
    while d < LANES:
        v = v + jnp.where(lane >= d, pltpu.roll(v, d, axis=1), 0.0)
        d *= 2
    tot = jnp.broadcast_to(v[:, LANES - 1:LANES], v.shape)
    inc = tot
    d = 1
    while d < r:
        inc = inc + jnp.where(row >= d, pltpu.roll(inc, d, axis=0), 0.0)
        d *= 2
    o_ref[0] = v + (inc - tot)


def _fcum(fl_t, bf_rows):
    bh, r, _ = fl_t.shape
    return pl.pallas_call(
        _fcum_kernel,
        grid=(bh,),
        in_specs=[
            pl.BlockSpec((1, r, LANES), lambda i: (i, 0, 0)),
            pl.BlockSpec((1, 1, LANES), lambda i: (i, 0, 0)),
        ],
        out_specs=pl.BlockSpec((1, r, LANES), lambda i: (i, 0, 0)),
        out_shape=jax.ShapeDtypeStruct((bh, r, LANES), F32),
        compiler_params=_cparams(("parallel",)),
        name="fcum",
    )(fl_t, bf_rows)


LOG2E = 1.4426950408889634
N_PIECES = 3


def _bf16_pieces(x):
    out = []
    for _ in range(N_PIECES):
        p = x.astype(BF16).astype(F32)
        out.append(p)
        x = x - p
    return out


def _head_norm(x, gain, lane):
    x2 = x * x
    lo = lane < FOX_HEAD_DIM
    ss_a = jnp.sum(jnp.where(lo, x2, 0.0), axis=-1, keepdims=True)
    ss_b = jnp.sum(jnp.where(lo, 0.0, x2), axis=-1, keepdims=True)
    ms = jnp.where(lo, ss_a, ss_b) * (1.0 / FOX_HEAD_DIM)
    return x * lax.rsqrt(ms + EPS) * gain


def _attn_kernel(q_ref, k_ref, v_ref, o_ref, f_ref, qg_ref, kg_ref, out_ref,
                 kx_ref, vt_ref, qt_ref, m_ref, acc_ref, sta_ref, stb_ref, *, tq, tu, prep_rows):
    qi = pl.program_id(2)
    s_len = k_ref.shape[1]
    half = FOX_HEAD_DIM

    @pl.when(qi == 0)
    def _prepare_kv():
        lane = lax.broadcasted_iota(jnp.int32, (prep_rows, LANES), 1)
        rowt = lax.broadcasted_iota(jnp.int32, (LANES, prep_rows), 0)

        def body(c, carry):
            r0 = pl.multiple_of(c * prep_rows, prep_rows)
            kn = _head_norm(k_ref[0, pl.ds(r0, prep_rows), :].astype(F32), kg_ref[...], lane)
            kx_ref[pl.ds(r0, prep_rows), 0:LANES] = kn.astype(BF16)
            ext = jnp.zeros((LANES, prep_rows), F32)
            for hh in range(2):
                f2 = f_ref[0, 0, hh:hh + 1, pl.ds(r0, prep_rows)] * (-LOG2E)
                for j, piece in enumerate(_bf16_pieces(f2)):
                    ext = jnp.where(rowt == hh * N_PIECES + j, piece, ext)
            kx_ref[pl.ds(r0, prep_rows), LANES:2 * LANES] = ext.T.astype(BF16)
            vt = v_ref[0, pl.ds(r0, prep_rows), :].astype(F32).T
            vt_ref[0, :, pl.ds(r0, prep_rows)] = jnp.where(rowt < half, vt, 1.0).astype(BF16)
            vt_ref[1, :, pl.ds(r0, prep_rows)] = jnp.where(rowt < half, 1.0, vt).astype(BF16)
            return carry

        lax.fori_loop(0, s_len // prep_rows, body, 0)

    lane = lax.broadcasted_iota(jnp.int32, (tq, LANES), 1)
    qn = _head_norm(q_ref[0].astype(F32), qg_ref[...], lane) * (FOX_HEAD_DIM ** -0.5 * LOG2E)
    qt = qn.T
    rowq = lax.broadcasted_iota(jnp.int32, (LANES, tq), 0)
    qt_ref[0:LANES, 0:tq] = jnp.where(rowq < half, qt, 0.0).astype(BF16)
    qt_ref[0:LANES, tq:2 * tq] = jnp.where(rowq < half, 0.0, qt).astype(BF16)
    qt_ref[LANES:2 * LANES, 0:tq] = jnp.where(rowq < N_PIECES, 1.0, 0.0).astype(BF16)
    qt_ref[LANES:2 * LANES, tq:2 * tq] = jnp.where(
        (rowq >= N_PIECES) & (rowq < 2 * N_PIECES), 1.0, 0.0).astype(BF16)
    m_ref[...] = jnp.full(m_ref.shape, NEG_BIG, F32)
    acc_ref[...] = jnp.zeros(acc_ref.shape, F32)

    nqh = tq // tu
    units = [(hh, qh) for hh in range(2) for qh in range(nqh)]
    bufs = (sta_ref, stb_ref)

    def logits(c0, unit):
        col = unit[0] * tq + unit[1] * tu
        return jnp.dot(kx_ref[pl.ds(c0, tu), :], qt_ref[:, col:col + tu],
                       preferred_element_type=F32)

    def accumulate(st_ref, c0, unit, triangular):
        st = st_ref[...]
        if triangular:
            key = lax.broadcasted_iota(jnp.int32, (tu, tu), 0)
            qry = lax.broadcasted_iota(jnp.int32, (tu, tu), 1)
            st = jnp.where(key <= qry, st, NEG_BIG)
        col = unit[0] * tq + unit[1] * tu
        cols = slice(col, col + tu)
        m_old = m_ref[:, cols]
        m_new = jnp.maximum(m_old, jnp.max(st, axis=0, keepdims=True))
        alpha = jnp.exp2(m_old - m_new)
        pt = jnp.exp2(st - m_new).astype(BF16)
        m_ref[:, cols] = m_new
        pv = jnp.dot(vt_ref[unit[0], :, pl.ds(c0, tu)], pt, preferred_element_type=F32)
        acc_ref[:, cols] = acc_ref[:, cols] * alpha + pv

    sta_ref[...] = logits(0, units[0])

    def body(kb, carry):
        c0 = pl.multiple_of(kb * tu, tu)
        for i, unit in enumerate(units):
            if i + 1 < len(units):
                bufs[(i + 1) % 2][...] = logits(c0, units[i + 1])
            else:
                bufs[(i + 1) % 2][...] = logits(c0 + tu, units[0])
            accumulate(bufs[i % 2], c0, unit, False)
        return carry

    def body_pair(j, carry):
        return body(2 * j + 1, body(2 * j, carry))

    n_off = qi * nqh
    n_pairs = lax.shift_right_logical(n_off, 1)
    lax.fori_loop(0, n_pairs, body_pair, 0)
    lax.fori_loop(2 * n_pairs, n_off, body, 0)

    cd = pl.multiple_of(qi * tq, tq)
    diag = [(j, (hh, qh)) for j in range(nqh) for hh in range(2) for qh in range(j, nqh)]
    for i, (j, unit) in enumerate(diag):
        if i + 1 < len(diag):
            jn, un = diag[i + 1]
            bufs[(i + 1) % 2][...] = logits(cd + jn * tu, un)
        accumulate(bufs[i % 2], cd + j * tu, unit, unit[1] == j)

    acc = acc_ref[...]
    res = jnp.concatenate([acc[0:half, 0:tq] / acc[half:, 0:tq],
                           acc[half:, tq:] / acc[0:half, tq:]], axis=0)
    out_ref[0] = (res.T * _sigmoid(o_ref[0].astype(F32))).astype(BF16)


def _attention(qkvo, f_pairs, qg2, kg2, tq):
    bsz, s, n = qkvo.shape
    d = n // 4
    npairs = d // LANES
    tu = 512
    kern = functools.partial(_attn_kernel, tq=tq, tu=tu, prep_rows=tu)
    return pl.pallas_call(
        kern,
        grid=(bsz, npairs, s // tq),
        in_specs=[
            pl.BlockSpec((1, tq, LANES), lambda b, h, i: (b, i, h)),
            pl.BlockSpec((1, s, LANES), lambda b, h, i: (b, 0, npairs + h)),
            pl.BlockSpec((1, s, LANES), lambda b, h, i: (b, 0, 2 * npairs + h)),
            pl.BlockSpec((1, tq, LANES), lambda b, h, i: (b, i, 3 * npairs + h)),
            pl.BlockSpec((1, 1, 2, s), lambda b, h, i: (b, h, 0, 0)),
            _const_spec((1, LANES)),
            _const_spec((1, LANES)),
        ],
        out_specs=pl.BlockSpec((1, tq, LANES), lambda b, h, i: (b, i, h)),
        out_shape=jax.ShapeDtypeStruct((bsz, s, d), BF16),
        scratch_shapes=[
            pltpu.VMEM((s, 2 * LANES), BF16),
            pltpu.VMEM((2, LANES, s), BF16),
            pltpu.VMEM((2 * LANES, 2 * tq), BF16),
            pltpu.VMEM((1, 2 * tq), F32),
            pltpu.VMEM((LANES, 2 * tq), F32),
            pltpu.VMEM((tu, tu), F32),
            pltpu.VMEM((tu, tu), F32),
        ],
        compiler_params=_cparams(("parallel", "parallel", "arbitrary")),
        name="fox_attn",
    )(qkvo, qkvo, qkvo, qkvo, f_pairs, qg2, kg2)


def _shift_rows(a, d, prev_tail):
    r = pltpu.roll(a, d, axis=0)
    row = lax.broadcasted_iota(jnp.int32, prev_tail.shape, 0)
    top = jnp.where(row < d, pltpu.roll(prev_tail, d, axis=0), r[:SUBLANES])
    return jnp.concatenate([top, r[SUBLANES:]], axis=0)


def _ffn_kernel(*refs, cw, fuse_attn_out, final_norm):
    refs = list(refs)
    x_ref = refs.pop(0)
    if fuse_attn_out:
        ao_ref, wo_ref, g1_ref = refs.pop(0), refs.pop(0), refs.pop(0)
    g_ref, sh_ref, sc_ref, g2_ref, wup_ref, cw_ref, cb_ref, wdn_ref = refs[:8]
    refs = refs[8:]
    if final_norm:
        fg_ref = refs.pop(0)
    o_ref, tail_ref, act_ref = refs

    si = pl.program_id(1)
    tm = x_ref.shape[1]
    dff = wdn_ref.shape[0]

    @pl.when(si == 0)
    def _zero_tail():
        tail_ref[...] = jnp.zeros_like(tail_ref)

    x = x_ref[0]
    if fuse_attn_out:
        x = x + g1_ref[0] * jnp.dot(ao_ref[0], wo_ref[...], preferred_element_type=F32)
    hb = _norm_mod(x, g_ref[...], sh_ref[0], sc_ref[0]).astype(BF16)

    def conv(c0):
        a = jnp.dot(hb, wup_ref[:, c0:c0 + cw], preferred_element_type=F32)
        prev = tail_ref[:, c0:c0 + cw]
        tail_ref[:, c0:c0 + cw] = a[tm - SUBLANES:]
        w = cw_ref[:, c0:c0 + cw]
        return (_shift_rows(a, 2, prev) * w[0:1] + _shift_rows(a, 1, prev) * w[1:2]
                + a * w[2:3] + cb_ref[:, c0:c0 + cw])

    for c0 in range(0, dff, cw):
        gate = conv(c0)
        val = conv(dff + c0)
        act_ref[:, c0:c0 + cw] = (gate * _sigmoid(gate) * val).astype(BF16)

    y = jnp.dot(act_ref[...], wdn_ref[...], preferred_element_type=F32)
    xo = x + g2_ref[0] * y
    if final_norm:
        ms = jnp.mean(xo * xo, axis=-1, keepdims=True)
        xo = xo * lax.rsqrt(ms + EPS) * fg_ref[...]
    o_ref[0] = xo


def _ffn(x, attn_out, norm_g, sh, sc, g2, w_up, conv_w, conv_b, w_down, final_g, tm):
    bsz, s, d = x.shape
    dff = w_down.shape[0]
    fuse_attn_out = attn_out is not None
    final_norm = final_g is not None
    row_spec = pl.BlockSpec((1, tm, d), lambda b, i: (b, i, 0))
    vec_spec = pl.BlockSpec((1, 1, d), lambda b, i: (b, 0, 0))
    args, specs = [x], [row_spec]
    if fuse_attn_out:
        ao, w_out, g1 = attn_out
        args += [ao, w_out, g1]
        specs += [row_spec, _const_spec((d, d)), vec_spec]
    args += [norm_g, sh, sc, g2, w_up, conv_w, conv_b, w_down]
    specs += [_const_spec((1, d)), vec_spec, vec_spec, vec_spec,
              _const_spec((d, 2 * dff)), _const_spec((CONV_WIDTH, 2 * dff)),
              _const_spec((1, 2 * dff)), _const_spec((dff, d))]
    if final_norm:
        args.append(final_g)
        specs.append(_const_spec((1, d)))
    kern = functools.partial(_ffn_kernel, cw=256, fuse_attn_out=fuse_attn_out,
                             final_norm=final_norm)
    return pl.pallas_call(
        kern,
        grid=(bsz, s // tm),
        in_specs=specs,
        out_specs=row_spec,
        out_shape=jax.ShapeDtypeStruct((bsz, s, d), F32),
        scratch_shapes=[
            pltpu.VMEM((SUBLANES, 2 * dff), F32),
            pltpu.VMEM((tm, dff), BF16),
        ],
        compiler_params=_cparams(("parallel", "arbitrary")),
        name="conv_ffn",
    )(*args)


def _gelu_tanh(x):
    return 0.5 * x * (1.0 + jnp.tanh(0.7978845608028654 * (x + 0.044715 * (x * x * x))))


def _sgu_kernel(x_ref, g_ref, sh_ref, sc_ref, g1_ref, win_ref, bin_ref, vg_ref, vb_ref,
                ws_ref, bs_ref, wout_ref, o_ref, v_ref, vn_ref, y_ref):
    tm = x_ref.shape[1]
    width = vg_ref.shape[1]
    gdim = width // SGU_GROUPS
    x = x_ref[0]
    hb = _norm_mod(x, g_ref[...], sh_ref[0], sc_ref[0]).astype(BF16)

    def z_cols(c0, n):
        return _gelu_tanh(jnp.dot(hb, win_ref[:, c0:c0 + n], preferred_element_type=F32)
                          + bin_ref[:, c0:c0 + n])

    vcw = 512
    for c0 in range(0, width, vcw):
        v_ref[:, c0:c0 + vcw] = z_cols(width + c0, vcw)
    v = v_ref[...]
    mu = jnp.mean(v, axis=-1, keepdims=True)
    dv = v - mu
    var = jnp.mean(dv * dv, axis=-1, keepdims=True)
    vn_ref[...] = (dv * lax.rsqrt(var + EPS) * vg_ref[...] + vb_ref[...]).astype(BF16)

    t = lax.broadcasted_iota(jnp.int32, (SGU_BLOCK, SGU_BLOCK), 0)
    s = lax.broadcasted_iota(jnp.int32, (SGU_BLOCK, SGU_BLOCK), 1)
    allowed = (s // CHUNK) <= (t // CHUNK)
    for g in range(SGU_GROUPS):
        c0 = g * gdim
        u = z_cols(c0, gdim)
        ws = jnp.where(allowed, ws_ref[g], 0.0).astype(BF16)
        bias = jnp.concatenate([bs_ref[g]] * (gdim // LANES), axis=1)
        for r0 in range(0, tm, SGU_BLOCK):
            mixed = jnp.dot(ws, vn_ref[r0:r0 + SGU_BLOCK, c0:c0 + gdim],
                            preferred_element_type=F32) + bias
            y_ref[r0:r0 + SGU_BLOCK, c0:c0 + gdim] = (u[r0:r0 + SGU_BLOCK] * mixed).astype(BF16)

    y = jnp.dot(y_ref[...], wout_ref[...], preferred_element_type=F32)
    o_ref[0] = x + g1_ref[0] * y


def _sgu(x, norm_g, sh, sc, g1, w_in, b_in, v_gain, v_bias, w_s, bs_b, w_out, tm):
    bsz, s, d = x.shape
    width = w_out.shape[0]
    row_spec = pl.BlockSpec((1, tm, d), lambda b, i: (b, i, 0))
    vec_spec = pl.BlockSpec((1, 1, d), lambda b, i: (b, 0, 0))
    return pl.pallas_call(
        _sgu_kernel,
        grid=(bsz, s // tm),
        in_specs=[
            row_spec, _const_spec((1, d)), vec_spec, vec_spec, vec_spec,
            _const_spec((d, 2 * width)), _const_spec((1, 2 * width)),
            _const_spec((1, width)), _const_spec((1, width)),
            _const_spec((SGU_GROUPS, SGU_BLOCK, SGU_BLOCK)),
            _const_spec((SGU_GROUPS, SGU_BLOCK, LANES)),
            _const_spec((width, d)),
        ],
        out_specs=row_spec,
        out_shape=jax.ShapeDtypeStruct((bsz, s, d), F32),
        scratch_shapes=[
            pltpu.VMEM((tm, width), F32),
            pltpu.VMEM((tm, width), BF16),
            pltpu.VMEM((tm, width), BF16),
        ],
        compiler_params=_cparams(("parallel", "parallel")),
        name="sgu",
    )(x, norm_g, sh, sc, g1, w_in, b_in, v_gain, v_bias, w_s, bs_b, w_out)


def kernel(x, c, fox_w_in, fox_b_f, fox_q_gain, fox_k_gain, fox_w_out, sgu_w_in, sgu_b_in,
           sgu_v_gain, sgu_v_bias, sgu_w_s, sgu_b_s, sgu_w_out, ffn_w_up, ffn_conv_w,
           ffn_conv_b, ffn_w_down, ada_w, ada_b, norm1_g, norm2_g, final_g):
    bsz, s, d = x.shape
    depth = ada_w.shape[0]
    assert depth == 2 and d == FOX_HEADS * FOX_HEAD_DIM
    assert s % 512 == 0
    tm = 512
    tq = 1024 if s % 1024 == 0 else 512

    mod = _ada_mod(c, ada_w, ada_b)
    mods = [[mod[l, :, k * d:(k + 1) * d].reshape(bsz, 1, d) for k in range(6)]
            for l in range(depth)]

    sh1, sc1, g1, sh2, sc2, g2 = mods[0]
    w_in = fox_w_in[0]
    w_main = w_in[:, :4 * d].astype(BF16)
    w_fl = jnp.zeros((d, LANES), BF16).at[:, :FOX_HEADS].set(w_in[:, 4 * d:].astype(BF16))
    qkvo, fl = _fox_in(x, norm1_g[0].reshape(1, d), sh1, sc1, w_main, w_fl, tm)

    fl_t = jnp.transpose(fl[:, :, :FOX_HEADS], (0, 2, 1)).reshape(
        bsz * FOX_HEADS, s // LANES, LANES)
    bf_rows = jnp.broadcast_to(jnp.tile(fox_b_f[0], bsz)[:, None, None],
                               (bsz * FOX_HEADS, 1, LANES))
    f_pairs = _fcum(fl_t, bf_rows).reshape(bsz, FOX_HEADS // 2, 2, s)

    qg2 = jnp.tile(fox_q_gain[0], 2).reshape(1, LANES)
    kg2 = jnp.tile(fox_k_gain[0], 2).reshape(1, LANES)
    ao = _attention(qkvo, f_pairs, qg2, kg2, tq)

    x = _ffn(x, (ao, fox_w_out[0].astype(BF16), g1), norm2_g[0].reshape(1, d), sh2, sc2, g2,
             ffn_w_up[0].astype(BF16), ffn_conv_w[0], ffn_conv_b[0].reshape(1, -1),
             ffn_w_down[0].astype(BF16), None, tm)

    sh1, sc1, g1, sh2, sc2, g2 = mods[1]
    width = sgu_w_out.shape[1]
    bs_b = jnp.broadcast_to(sgu_b_s[0][:, :, None], (SGU_GROUPS, SGU_BLOCK, LANES))
    x = _sgu(x, norm1_g[1].reshape(1, d), sh1, sc1, g1, sgu_w_in[0].astype(BF16),
             sgu_b_in[0].reshape(1, -1), sgu_v_gain[0].reshape(1, width),
             sgu_v_bias[0].reshape(1, width), sgu_w_s[0], bs_b, sgu_w_out[0].astype(BF16), tm)
    x = _ffn(x, None, norm2_g[1].reshape(1, d), sh2, sc2, g2,
             ffn_w_up[1].astype(BF16), ffn_conv_w[1], ffn_conv_b[1].reshape(1, -1),
             ffn_w_down[1].astype(BF16), final_g.reshape(1, d), tm)
    return x
```

```python
import functools

import jax
import jax.numpy as jnp
from jax import lax
from jax.experimental import pallas as pl
from jax.experimental.pallas import tpu as pltpu

F32 = jnp.float32
BF16 = jnp.bfloat16

FOX_HEADS = 16
FOX_HEAD_DIM = 64
SGU_GROUPS = 8
SGU_BLOCK = 128
CHUNK = 64
CONV_WIDTH = 3
EPS = 1e-6

LANES = 128
SUBLANES = 8
VMEM_LIMIT = 56 * 1024 * 1024
NEG_BIG = -1e30


def _cparams(sem):
    return pltpu.CompilerParams(dimension_semantics=sem, vmem_limit_bytes=VMEM_LIMIT)


def _const_spec(shape):
    nd = len(shape)
    return pl.BlockSpec(shape, lambda *_: (0,) * nd, pipeline_mode=pl.Buffered(1))


def _sigmoid(x):
    return 1.0 / (1.0 + jnp.exp(-x))


def _norm_mod(x, g, sh, sc):
    ms = jnp.mean(x * x, axis=-1, keepdims=True)
    h = x * lax.rsqrt(ms + EPS) * g
    return h * (1.0 + sc) + sh


def _mod_kernel(c_ref, w_ref, b_ref, o_ref):
    c = c_ref[...]
    ca = c * _sigmoid(c)
    o_ref[0] = jnp.dot(ca, w_ref[0], preferred_element_type=F32,
                       precision=lax.Precision.HIGHEST) + b_ref[0]


def _ada_mod(c, ada_w, ada_b):
    depth, d, n = ada_w.shape
    bsz = c.shape[0]
    rows = -(-bsz // SUBLANES) * SUBLANES
    cp = jnp.zeros((rows, d), F32).at[:bsz].set(c)
    tn = 1024
    out = pl.pallas_call(
        _mod_kernel,
        grid=(depth, n // tn),
        in_specs=[
            pl.BlockSpec((rows, d), lambda l, j: (0, 0)),
            pl.BlockSpec((1, d, tn), lambda l, j: (l, 0, j)),
            pl.BlockSpec((1, 1, tn), lambda l, j: (l, 0, j)),
        ],
        out_specs=pl.BlockSpec((1, rows, tn), lambda l, j: (l, 0, j)),
        out_shape=jax.ShapeDtypeStruct((depth, rows, n), F32),
        compiler_params=_cparams(("parallel", "parallel")),
        name="ada_mod",
    )(cp, ada_w, ada_b.reshape(depth, 1, n))
    return out[:, :bsz]


def _fox_in_kernel(x_ref, g_ref, sh_ref, sc_ref, w_ref, wfl_ref, o_ref, fl_ref, *, tn):
    hb = _norm_mod(x_ref[0], g_ref[...], sh_ref[0], sc_ref[0]).astype(BF16)
    n = w_ref.shape[1]
    for j in range(0, n, tn):
        o_ref[0, :, j:j + tn] = jnp.dot(
            hb, w_ref[:, j:j + tn], preferred_element_type=F32).astype(BF16)
    fl_ref[0] = jnp.dot(hb, wfl_ref[...], preferred_element_type=F32)


def _fox_in(x, g, sh, sc, w, wfl, tm):
    bsz, s, d = x.shape
    n = w.shape[1]
    kern = functools.partial(_fox_in_kernel, tn=1024)
    return pl.pallas_call(
        kern,
        grid=(bsz, s // tm),
        in_specs=[
            pl.BlockSpec((1, tm, d), lambda b, i: (b, i, 0)),
            _const_spec((1, d)),
            pl.BlockSpec((1, 1, d), lambda b, i: (b, 0, 0)),
            pl.BlockSpec((1, 1, d), lambda b, i: (b, 0, 0)),
            _const_spec((d, n)),
            _const_spec((d, LANES)),
        ],
        out_specs=[
            pl.BlockSpec((1, tm, n), lambda b, i: (b, i, 0)),
            pl.BlockSpec((1, tm, LANES), lambda b, i: (b, i, 0)),
        ],
        out_shape=[
            jax.ShapeDtypeStruct((bsz, s, n), BF16),
            jax.ShapeDtypeStruct((bsz, s, LANES), F32),
        ],
        compiler_params=_cparams(("parallel", "parallel")),
        name="fox_in",
    )(x, g, sh, sc, w, wfl)


def _fcum_kernel(fl_ref, bf_ref, o_ref):
    r = fl_ref.shape[1]
    lane = lax.broadcasted_iota(jnp.int32, (r, LANES), 1)
    row = lax.broadcasted_iota(jnp.int32, (r, LANES), 0)
    for h in range(fl_ref.shape[0]):
        z = fl_ref[h] + bf_ref[h]
        v = jnp.minimum(z, 0.0) - jnp.log(1.0 + jnp.exp(-jnp.abs(z)))
        d = 1
        while d < LANES:
            v = v + jnp.where(lane >= d, pltpu.roll(v, d, axis=1), 0.0)
            d *= 2
        tot = jnp.broadcast_to(v[:, LANES - 1:LANES], v.shape)
        inc = tot
        d = 1
        while d < r:
            inc = inc + jnp.where(row >= d, pltpu.roll(inc, d, axis=0), 0.0)
            d *= 2
        o_ref[h] = v + (inc - tot)


def _fcum(fl_t, bf_rows, heads):
    bh, r, _ = fl_t.shape
    return pl.pallas_call(
        _fcum_kernel,
        grid=(bh // heads,),
        in_specs=[
            pl.BlockSpec((heads, r, LANES), lambda i: (i, 0, 0)),
            pl.BlockSpec((heads, 1, LANES), lambda i: (i, 0, 0)),
        ],
        out_specs=pl.BlockSpec((heads, r, LANES), lambda i: (i, 0, 0)),
        out_shape=jax.ShapeDtypeStruct((bh, r, LANES), F32),
        compiler_params=_cparams(("parallel",)),
        name="fcum",
    )(fl_t, bf_rows)


LOG2E = 1.4426950408889634
N_PIECES = 3
ONES_ROWS = 16


def _bf16_pieces(x):
    out = []
    for _ in range(N_PIECES):
        p = x.astype(BF16).astype(F32)
        out.append(p)
        x = x - p
    return out


def _head_norm(x, gain, lane):
    x2 = x * x
    lo = lane < FOX_HEAD_DIM
    ss_a = jnp.sum(jnp.where(lo, x2, 0.0), axis=-1, keepdims=True)
    ss_b = jnp.sum(jnp.where(lo, 0.0, x2), axis=-1, keepdims=True)
    ms = jnp.where(lo, ss_a, ss_b) * (1.0 / FOX_HEAD_DIM)
    return x * lax.rsqrt(ms + EPS) * gain


def _attn_kernel(q_ref, k_ref, v_ref, o_ref, f_ref, qg_ref, kg_ref, out_ref,
                 kx_ref, vt_ref, qt_ref, m_ref, acc_ref, sta_ref, stb_ref, *, tq, tu, prep_rows):
    qi = pl.program_id(2)
    s_len = k_ref.shape[1]
    half = FOX_HEAD_DIM

    @pl.when(qi == 0)
    def _prepare_kv():
        lane = lax.broadcasted_iota(jnp.int32, (prep_rows, LANES), 1)
        rowt = lax.broadcasted_iota(jnp.int32, (LANES, prep_rows), 0)

        def body(c, carry):
            r0 = pl.multiple_of(c * prep_rows, prep_rows)
            kn = _head_norm(k_ref[0, pl.ds(r0, prep_rows), :].astype(F32), kg_ref[...], lane)
            kx_ref[pl.ds(r0, prep_rows), 0:LANES] = kn.astype(BF16)
            ext = jnp.zeros((LANES, prep_rows), F32)
            for hh in range(2):
                f2 = f_ref[0, 0, hh:hh + 1, pl.ds(r0, prep_rows)] * (-LOG2E)
                for j, piece in enumerate(_bf16_pieces(f2)):
                    ext = jnp.where(rowt == hh * N_PIECES + j, piece, ext)
            kx_ref[pl.ds(r0, prep_rows), LANES:2 * LANES] = ext.T.astype(BF16)
            vt = v_ref[0, pl.ds(r0, prep_rows), :].astype(F32).T
            ones = jnp.ones((ONES_ROWS, prep_rows), BF16)
            for hh in range(2):
                vt_ref[hh, 0:half, pl.ds(r0, prep_rows)] = vt[hh * half:(hh + 1) * half].astype(BF16)
                vt_ref[hh, half:, pl.ds(r0, prep_rows)] = ones
            return carry

        lax.fori_loop(0, s_len // prep_rows, body, 0)

    lane = lax.broadcasted_iota(jnp.int32, (tq, LANES), 1)
    qn = _head_norm(q_ref[0].astype(F32), qg_ref[...], lane) * (FOX_HEAD_DIM ** -0.5 * LOG2E)
    qt = qn.T
    rowq = lax.broadcasted_iota(jnp.int32, (LANES, tq), 0)
    qt_ref[0:LANES, 0:tq] = jnp.where(rowq < half, qt, 0.0).astype(BF16)
    qt_ref[0:LANES, tq:2 * tq] = jnp.where(rowq < half, 0.0, qt).astype(BF16)
    qt_ref[LANES:2 * LANES, 0:tq] = jnp.where(rowq < N_PIECES, 1.0, 0.0).astype(BF16)
    qt_ref[LANES:2 * LANES, tq:2 * tq] = jnp.where(
        (rowq >= N_PIECES) & (rowq < 2 * N_PIECES), 1.0, 0.0).astype(BF16)
    m_ref[...] = jnp.full(m_ref.shape, NEG_BIG, F32)
    acc_ref[...] = jnp.zeros(acc_ref.shape, F32)

    nqh = tq // tu
    units = [(hh, qh) for hh in range(2) for qh in range(nqh)]
    bufs = (sta_ref, stb_ref)

    def logits(c0, unit):
        col = unit[0] * tq + unit[1] * tu
        return jnp.dot(kx_ref[pl.ds(c0, tu), :], qt_ref[:, col:col + tu],
                       preferred_element_type=F32)

    def accumulate(st_ref, c0, unit, triangular):
        st = st_ref[...]
        if triangular:
            key = lax.broadcasted_iota(jnp.int32, (tu, tu), 0)
            qry = lax.broadcasted_iota(jnp.int32, (tu, tu), 1)
            st = jnp.where(key <= qry, st, NEG_BIG)
        col = unit[0] * tq + unit[1] * tu
        cols = slice(col, col + tu)
        m_old = m_ref[:, cols]
        m_new = jnp.maximum(m_old, jnp.max(st, axis=0, keepdims=True))
        alpha = jnp.exp2(m_old - m_new)
        pt = jnp.exp2(st - m_new).astype(BF16)
        m_ref[:, cols] = m_new
        pv = jnp.dot(vt_ref[unit[0], :, pl.ds(c0, tu)], pt, preferred_element_type=F32)
        acc_ref[:, cols] = acc_ref[:, cols] * alpha + pv

    sta_ref[...] = logits(0, units[0])

    def body(kb, carry):
        c0 = pl.multiple_of(kb * tu, tu)
        for i, unit in enumerate(units):
            if i + 1 < len(units):
                bufs[(i + 1) % 2][...] = logits(c0, units[i + 1])
            else:
                bufs[(i + 1) % 2][...] = logits(c0 + tu, units[0])
            accumulate(bufs[i % 2], c0, unit, False)
        return carry

    def body_pair(j, carry):
        return body(2 * j + 1, body(2 * j, carry))

    n_off = qi * nqh
    n_pairs = lax.shift_right_logical(n_off, 1)
    lax.fori_loop(0, n_pairs, body_pair, 0)
    lax.fori_loop(2 * n_pairs, n_off, body, 0)

    cd = pl.multiple_of(qi * tq, tq)
    diag = [(j, (hh, qh)) for j in range(nqh) for hh in range(2) for qh in range(j, nqh)]
    for i, (j, unit) in enumerate(diag):
        if i + 1 < len(diag):
            jn, un = diag[i + 1]
            bufs[(i + 1) % 2][...] = logits(cd + jn * tu, un)
        accumulate(bufs[i % 2], cd + j * tu, unit, unit[1] == j)

    res = []
    for hh in range(2):
        cols = slice(hh * tq, (hh + 1) * tq)
        denom = acc_ref[half:half + SUBLANES, cols]
        res.append(acc_ref[0:half, cols] / jnp.concatenate([denom] * (half // SUBLANES), axis=0))
    res = jnp.concatenate(res, axis=0)
    out_ref[0] = (res.T * _sigmoid(o_ref[0].astype(F32))).astype(BF16)


def _attention(qkvo, f_pairs, qg2, kg2, tq):
    bsz, s, n = qkvo.shape
    d = n // 4
    npairs = d // LANES
    tu = 512
    kern = functools.partial(_attn_kernel, tq=tq, tu=tu, prep_rows=tu)
    return pl.pallas_call(
        kern,
        grid=(bsz, npairs, s // tq),
        in_specs=[
            pl.BlockSpec((1, tq, LANES), lambda b, h, i: (b, i, h)),
            pl.BlockSpec((1, s, LANES), lambda b, h, i: (b, 0, npairs + h)),
            pl.BlockSpec((1, s, LANES), lambda b, h, i: (b, 0, 2 * npairs + h)),
            pl.BlockSpec((1, tq, LANES), lambda b, h, i: (b, i, 3 * npairs + h)),
            pl.BlockSpec((1, 1, 2, s), lambda b, h, i: (b, h, 0, 0)),
            _const_spec((1, LANES)),
            _const_spec((1, LANES)),
        ],
        out_specs=pl.BlockSpec((1, tq, LANES), lambda b, h, i: (b, i, h)),
        out_shape=jax.ShapeDtypeStruct((bsz, s, d), BF16),
        scratch_shapes=[
            pltpu.VMEM((s, 2 * LANES), BF16),
            pltpu.VMEM((2, FOX_HEAD_DIM + ONES_ROWS, s), BF16),
            pltpu.VMEM((2 * LANES, 2 * tq), BF16),
            pltpu.VMEM((1, 2 * tq), F32),
            pltpu.VMEM((FOX_HEAD_DIM + ONES_ROWS, 2 * tq), F32),
            pltpu.VMEM((tu, tu), F32),
            pltpu.VMEM((tu, tu), F32),
        ],
        compiler_params=_cparams(("parallel", "parallel", "arbitrary")),
        name="fox_attn",
    )(qkvo, qkvo, qkvo, qkvo, f_pairs, qg2, kg2)


def _shift_rows(a, d, prev_tail):
    r = pltpu.roll(a, d, axis=0)
    row = lax.broadcasted_iota(jnp.int32, prev_tail.shape, 0)
    top = jnp.where(row < d, pltpu.roll(prev_tail, d, axis=0), r[:SUBLANES])
    return jnp.concatenate([top, r[SUBLANES:]], axis=0)


def _ffn_kernel(*refs, cw, fuse_attn_out, final_norm):
    refs = list(refs)
    x_ref = refs.pop(0)
    if fuse_attn_out:
        ao_ref, wo_ref, g1_ref = refs.pop(0), refs.pop(0), refs.pop(0)
    g_ref, sh_ref, sc_ref, g2_ref, wup_ref, cw_ref, cb_ref, wdn_ref = refs[:8]
    refs = refs[8:]
    if final_norm:
        fg_ref = refs.pop(0)
    o_ref, tail_ref, act_ref = refs

    si = pl.program_id(1)
    tm = x_ref.shape[1]
    dff = wdn_ref.shape[0]

    @pl.when(si == 0)
    def _zero_tail():
        tail_ref[...] = jnp.zeros_like(tail_ref)

    x = x_ref[0]
    if fuse_attn_out:
        x = x + g1_ref[0] * jnp.dot(ao_ref[0], wo_ref[...], preferred_element_type=F32)
    hb = _norm_mod(x, g_ref[...], sh_ref[0], sc_ref[0]).astype(BF16)

    def conv(c0):
        a = jnp.dot(hb, wup_ref[:, c0:c0 + cw], preferred_element_type=F32)
        prev = tail_ref[:, c0:c0 + cw]
        tail_ref[:, c0:c0 + cw] = a[tm - SUBLANES:]
        w = cw_ref[:, c0:c0 + cw]
        return (_shift_rows(a, 2, prev) * w[0:1] + _shift_rows(a, 1, prev) * w[1:2]
                + a * w[2:3] + cb_ref[:, c0:c0 + cw])

    for c0 in range(0, dff, cw):
        gate = conv(c0)
        val = conv(dff + c0)
        act_ref[:, c0:c0 + cw] = (gate * _sigmoid(gate) * val).astype(BF16)

    y = jnp.dot(act_ref[...], wdn_ref[...], preferred_element_type=F32)
    xo = x + g2_ref[0] * y
    if final_norm:
        ms = jnp.mean(xo * xo, axis=-1, keepdims=True)
        xo = xo * lax.rsqrt(ms + EPS) * fg_ref[...]
    o_ref[0] = xo


def _ffn(x, attn_out, norm_g, sh, sc, g2, w_up, conv_w, conv_b, w_down, final_g, tm):
    bsz, s, d = x.shape
    dff = w_down.shape[0]
    fuse_attn_out = attn_out is not None
    final_norm = final_g is not None
    row_spec = pl.BlockSpec((1, tm, d), lambda b, i: (b, i, 0))
    vec_spec = pl.BlockSpec((1, 1, d), lambda b, i: (b, 0, 0))
    args, specs = [x], [row_spec]
    if fuse_attn_out:
        ao, w_out, g1 = attn_out
        args += [ao, w_out, g1]
        specs += [row_spec, _const_spec((d, d)), vec_spec]
    args += [norm_g, sh, sc, g2, w_up, conv_w, conv_b, w_down]
    specs += [_const_spec((1, d)), vec_spec, vec_spec, vec_spec,
              _const_spec((d, 2 * dff)), _const_spec((CONV_WIDTH, 2 * dff)),
              _const_spec((1, 2 * dff)), _const_spec((dff, d))]
    if final_norm:
        args.append(final_g)
        specs.append(_const_spec((1, d)))
    kern = functools.partial(_ffn_kernel, cw=256, fuse_attn_out=fuse_attn_out,
                             final_norm=final_norm)
    return pl.pallas_call(
        kern,
        grid=(bsz, s // tm),
        in_specs=specs,
        out_specs=row_spec,
        out_shape=jax.ShapeDtypeStruct((bsz, s, d), F32),
        scratch_shapes=[
            pltpu.VMEM((SUBLANES, 2 * dff), F32),
            pltpu.VMEM((tm, dff), BF16),
        ],
        compiler_params=_cparams(("parallel", "arbitrary")),
        name="conv_ffn",
    )(*args)


GELU_C = 0.7978845608028654
GELU_A = 0.044715


def _gelu_tanh(x):
    inner = x * (GELU_C + (GELU_C * GELU_A) * (x * x))
    hx = 0.5 * x
    return hx + hx * jnp.tanh(inner)


def _sgu_kernel(x_ref, g_ref, sh_ref, sc_ref, g1_ref, win_ref, bin_ref, vg_ref, vb_ref,
                ws_ref, bs_ref, wout_ref, o_ref, v_ref, vn_ref, y_ref, u_ref):
    tm = x_ref.shape[1]
    width = vg_ref.shape[1]
    gdim = width // SGU_GROUPS
    x = x_ref[0]
    hb = _norm_mod(x, g_ref[...], sh_ref[0], sc_ref[0]).astype(BF16)

    def z_cols(c0, n):
        return _gelu_tanh(jnp.dot(hb, win_ref[:, c0:c0 + n], preferred_element_type=F32)
                          + bin_ref[:, c0:c0 + n])

    vcw = 512
    for c0 in range(0, width, vcw):
        v_ref[:, c0:c0 + vcw] = z_cols(width + c0, vcw)
    for c0 in range(0, width, vcw):
        u_ref[:, c0:c0 + vcw] = z_cols(c0, vcw)
    v = v_ref[...]
    mu = jnp.mean(v, axis=-1, keepdims=True)
    dv = v - mu
    var = jnp.mean(dv * dv, axis=-1, keepdims=True)
    vn_ref[...] = (dv * lax.rsqrt(var + EPS) * vg_ref[...] + vb_ref[...]).astype(BF16)

    t = lax.broadcasted_iota(jnp.int32, (SGU_BLOCK, SGU_BLOCK), 0)
    s = lax.broadcasted_iota(jnp.int32, (SGU_BLOCK, SGU_BLOCK), 1)
    allowed = (s // CHUNK) <= (t // CHUNK)
    for g in range(SGU_GROUPS):
        c0 = g * gdim
        ws = jnp.where(allowed, ws_ref[g], 0.0).astype(BF16)
        bias = jnp.concatenate([bs_ref[g]] * (gdim // LANES), axis=1)
        for r0 in range(0, tm, SGU_BLOCK):
            mixed = jnp.dot(ws, vn_ref[r0:r0 + SGU_BLOCK, c0:c0 + gdim],
                            preferred_element_type=F32) + bias
            y_ref[r0:r0 + SGU_BLOCK, c0:c0 + gdim] = (
                u_ref[r0:r0 + SGU_BLOCK, c0:c0 + gdim] * mixed).astype(BF16)

    y = jnp.dot(y_ref[...], wout_ref[...], preferred_element_type=F32)
    o_ref[0] = x + g1_ref[0] * y


def _sgu(x, norm_g, sh, sc, g1, w_in, b_in, v_gain, v_bias, w_s, bs_b, w_out, tm):
    bsz, s, d = x.shape
    width = w_out.shape[0]
    row_spec = pl.BlockSpec((1, tm, d), lambda b, i: (b, i, 0))
    vec_spec = pl.BlockSpec((1, 1, d), lambda b, i: (b, 0, 0))
    return pl.pallas_call(
        _sgu_kernel,
        grid=(bsz, s // tm),
        in_specs=[
            row_spec, _const_spec((1, d)), vec_spec, vec_spec, vec_spec,
            _const_spec((d, 2 * width)), _const_spec((1, 2 * width)),
            _const_spec((1, width)), _const_spec((1, width)),
            _const_spec((SGU_GROUPS, SGU_BLOCK, SGU_BLOCK)),
            _const_spec((SGU_GROUPS, SGU_BLOCK, LANES)),
            _const_spec((width, d)),
        ],
        out_specs=row_spec,
        out_shape=jax.ShapeDtypeStruct((bsz, s, d), F32),
        scratch_shapes=[
            pltpu.VMEM((tm, width), F32),
            pltpu.VMEM((tm, width), BF16),
            pltpu.VMEM((tm, width), BF16),
            pltpu.VMEM((tm, width), F32),
        ],
        compiler_params=_cparams(("parallel", "parallel")),
        name="sgu",
    )(x, norm_g, sh, sc, g1, w_in, b_in, v_gain, v_bias, w_s, bs_b, w_out)


def kernel(x, c, fox_w_in, fox_b_f, fox_q_gain, fox_k_gain, fox_w_out, sgu_w_in, sgu_b_in,
           sgu_v_gain, sgu_v_bias, sgu_w_s, sgu_b_s, sgu_w_out, ffn_w_up, ffn_conv_w,
           ffn_conv_b, ffn_w_down, ada_w, ada_b, norm1_g, norm2_g, final_g):
    bsz, s, d = x.shape
    depth = ada_w.shape[0]
    assert depth == 2 and d == FOX_HEADS * FOX_HEAD_DIM
    assert s % 512 == 0
    tm = 512
    tq = 2048 if s % 2048 == 0 else 512

    mod = _ada_mod(c, ada_w, ada_b)
    mods = [[mod[l, :, k * d:(k + 1) * d].reshape(bsz, 1, d) for k in range(6)]
            for l in range(depth)]

    sh1, sc1, g1, sh2, sc2, g2 = mods[0]
    w_in = fox_w_in[0]
    w_main = w_in[:, :4 * d].astype(BF16)
    w_fl = jnp.zeros((d, LANES), BF16).at[:, :FOX_HEADS].set(w_in[:, 4 * d:].astype(BF16))
    qkvo, fl = _fox_in(x, norm1_g[0].reshape(1, d), sh1, sc1, w_main, w_fl, tm)

    fl_t = jnp.transpose(fl[:, :, :FOX_HEADS], (0, 2, 1)).reshape(
        bsz * FOX_HEADS, s // LANES, LANES)
    bf_rows = jnp.broadcast_to(jnp.tile(fox_b_f[0], bsz)[:, None, None],
                               (bsz * FOX_HEADS, 1, LANES))
    f_pairs = _fcum(fl_t, bf_rows, FOX_HEADS).reshape(bsz, FOX_HEADS // 2, 2, s)

    qg2 = jnp.tile(fox_q_gain[0], 2).reshape(1, LANES)
    kg2 = jnp.tile(fox_k_gain[0], 2).reshape(1, LANES)
    ao = _attention(qkvo, f_pairs, qg2, kg2, tq)

    x = _ffn(x, (ao, fox_w_out[0].astype(BF16), g1), norm2_g[0].reshape(1, d), sh2, sc2, g2,
             ffn_w_up[0].astype(BF16), ffn_conv_w[0], ffn_conv_b[0].reshape(1, -1),
             ffn_w_down[0].astype(BF16), None, tm)

    sh1, sc1, g1, sh2, sc2, g2 = mods[1]
    width = sgu_w_out.shape[1]
    bs_b = jnp.broadcast_to(sgu_b_s[0][:, :, None], (SGU_GROUPS, SGU_BLOCK, LANES))
    x = _sgu(x, norm1_g[1].reshape(1, d), sh1, sc1, g1, sgu_w_in[0].astype(BF16),
             sgu_b_in[0].reshape(1, -1), sgu_v_gain[0].reshape(1, width),
             sgu_v_bias[0].reshape(1, width), sgu_w_s[0], bs_b, sgu_w_out[0].astype(BF16), tm)
    x = _ffn(x, None, norm2_g[1].reshape(1, d), sh2, sc2, g2,
             ffn_w_up[1].astype(BF16), ffn_conv_w[1], ffn_conv_b[1].reshape(1, -1),
             ffn_w_down[1].astype(BF16), final_g.reshape(1, d), tm)
    return x
```

```python
import functools

import jax
import jax.numpy as jnp
from jax import lax
from jax.experimental import pallas as pl
from jax.experimental.pallas import tpu as pltpu

F32 = jnp.float32
BF16 = jnp.bfloat16

FOX_HEADS = 16
FOX_HEAD_DIM = 64
SGU_GROUPS = 8
SGU_BLOCK = 128
CHUNK = 64
CONV_WIDTH = 3
EPS = 1e-6

LANES = 128
SUBLANES = 8
VMEM_LIMIT = 56 * 1024 * 1024
NEG_BIG = -1e30


def _cparams(sem):
    return pltpu.CompilerParams(dimension_semantics=sem, vmem_limit_bytes=VMEM_LIMIT)


def _const_spec(shape):
    nd = len(shape)
    return pl.BlockSpec(shape, lambda *_: (0,) * nd, pipeline_mode=pl.Buffered(1))


def _sigmoid(x):
    return 1.0 / (1.0 + jnp.exp(-x))


def _dot_row_halves(a, w):
    hm = a.shape[0] // 2
    return jnp.concatenate([jnp.dot(a[:hm], w, preferred_element_type=F32),
                            jnp.dot(a[hm:], w, preferred_element_type=F32)], axis=0)


def _norm_mod(x, g, sh, sc):
    ms = jnp.mean(x * x, axis=-1, keepdims=True)
    h = x * lax.rsqrt(ms + EPS) * g
    return h * (1.0 + sc) + sh


def _mod_kernel(c_ref, w_ref, b_ref, o_ref):
    c = c_ref[...]
    ca = c * _sigmoid(c)
    o_ref[0] = jnp.dot(ca, w_ref[0], preferred_element_type=F32,
                       precision=lax.Precision.HIGHEST) + b_ref[0]


def _ada_mod(c, ada_w, ada_b):
    depth, d, n = ada_w.shape
    bsz = c.shape[0]
    rows = -(-bsz // SUBLANES) * SUBLANES
    cp = jnp.zeros((rows, d), F32).at[:bsz].set(c)
    tn = 1024
    out = pl.pallas_call(
        _mod_kernel,
        grid=(depth, n // tn),
        in_specs=[
            pl.BlockSpec((rows, d), lambda l, j: (0, 0)),
            pl.BlockSpec((1, d, tn), lambda l, j: (l, 0, j)),
            pl.BlockSpec((1, 1, tn), lambda l, j: (l, 0, j)),
        ],
        out_specs=pl.BlockSpec((1, rows, tn), lambda l, j: (l, 0, j)),
        out_shape=jax.ShapeDtypeStruct((depth, rows, n), F32),
        compiler_params=_cparams(("parallel", "parallel")),
        name="ada_mod",
    )(cp, ada_w, ada_b.reshape(depth, 1, n))
    return out[:, :bsz]


def _fox_in_kernel(x_ref, g_ref, sh_ref, sc_ref, w_ref, wfl_ref, o_ref, fl_ref, *, tn):
    hb = _norm_mod(x_ref[0], g_ref[...], sh_ref[0], sc_ref[0]).astype(BF16)
    n = w_ref.shape[1]
    for j in range(0, n, tn):
        o_ref[0, :, j:j + tn] = jnp.dot(
            hb, w_ref[:, j:j + tn], preferred_element_type=F32).astype(BF16)
    flt = lax.dot_general(wfl_ref[...], hb, (((1,), (1,)), ((), ())),
                          preferred_element_type=F32)
    fl_ref[0] = flt[0:fl_ref.shape[1]]


def _fox_in(x, g, sh, sc, w, wfl, tm):
    bsz, s, d = x.shape
    n = w.shape[1]
    kern = functools.partial(_fox_in_kernel, tn=1024)
    return pl.pallas_call(
        kern,
        grid=(bsz, s // tm),
        in_specs=[
            pl.BlockSpec((1, tm, d), lambda b, i: (b, i, 0)),
            _const_spec((1, d)),
            pl.BlockSpec((1, 1, d), lambda b, i: (b, 0, 0)),
            pl.BlockSpec((1, 1, d), lambda b, i: (b, 0, 0)),
            _const_spec((d, n)),
            _const_spec((LANES, d)),
        ],
        out_specs=[
            pl.BlockSpec((1, tm, n), lambda b, i: (b, i, 0)),
            pl.BlockSpec((1, FOX_HEADS, tm), lambda b, i: (b, 0, i)),
        ],
        out_shape=[
            jax.ShapeDtypeStruct((bsz, s, n), BF16),
            jax.ShapeDtypeStruct((bsz, FOX_HEADS, s), F32),
        ],
        compiler_params=_cparams(("parallel", "parallel")),
        name="fox_in",
    )(x, g, sh, sc, w, wfl)


def _fcum_kernel(fl_ref, bf_ref, o_ref):
    r = fl_ref.shape[1]
    lane = lax.broadcasted_iota(jnp.int32, (r, LANES), 1)
    row = lax.broadcasted_iota(jnp.int32, (r, LANES), 0)
    for h in range(fl_ref.shape[0]):
        z = fl_ref[h] + bf_ref[h]
        v = jnp.minimum(z, 0.0) - jnp.log(1.0 + jnp.exp(-jnp.abs(z)))
        d = 1
        while d < LANES:
            v = v + jnp.where(lane >= d, pltpu.roll(v, d, axis=1), 0.0)
            d *= 2
        tot = jnp.broadcast_to(v[:, LANES - 1:LANES], v.shape)
        inc = tot
        d = 1
        while d < r:
            inc = inc + jnp.where(row >= d, pltpu.roll(inc, d, axis=0), 0.0)
            d *= 2
        o_ref[h] = v + (inc - tot)


def _fcum(fl_t, bf_rows, heads):
    bh, r, _ = fl_t.shape
    return pl.pallas_call(
        _fcum_kernel,
        grid=(bh // heads,),
        in_specs=[
            pl.BlockSpec((heads, r, LANES), lambda i: (i, 0, 0)),
            pl.BlockSpec((heads, 1, LANES), lambda i: (i, 0, 0)),
        ],
        out_specs=pl.BlockSpec((heads, r, LANES), lambda i: (i, 0, 0)),
        out_shape=jax.ShapeDtypeStruct((bh, r, LANES), F32),
        compiler_params=_cparams(("parallel",)),
        name="fcum",
    )(fl_t, bf_rows)


LOG2E = 1.4426950408889634
N_PIECES = 3
ONES_ROWS = 16


def _bf16_pieces(x):
    out = []
    for _ in range(N_PIECES):
        p = x.astype(BF16).astype(F32)
        out.append(p)
        x = x - p
    return out


def _head_norm(x, gain, lane):
    x2 = x * x
    lo = lane < FOX_HEAD_DIM
    ss_a = jnp.sum(jnp.where(lo, x2, 0.0), axis=-1, keepdims=True)
    ss_b = jnp.sum(jnp.where(lo, 0.0, x2), axis=-1, keepdims=True)
    ms = jnp.where(lo, ss_a, ss_b) * (1.0 / FOX_HEAD_DIM)
    return x * lax.rsqrt(ms + EPS) * gain


def _attn_kernel(q_ref, k_ref, v_ref, o_ref, f_ref, qg_ref, kg_ref, out_ref,
                 kx_ref, vt_ref, qt_ref, m_ref, acc_ref, sta_ref, stb_ref, *, tq, tu, prep_rows):
    qi = pl.program_id(2)
    s_len = k_ref.shape[1]
    half = FOX_HEAD_DIM

    @pl.when(qi == 0)
    def _prepare_kv():
        lane = lax.broadcasted_iota(jnp.int32, (prep_rows, LANES), 1)
        rowt = lax.broadcasted_iota(jnp.int32, (LANES, prep_rows), 0)

        def body(c, carry):
            r0 = pl.multiple_of(c * prep_rows, prep_rows)
            kn = _head_norm(k_ref[0, pl.ds(r0, prep_rows), :].astype(F32), kg_ref[...], lane)
            kx_ref[pl.ds(r0, prep_rows), 0:LANES] = kn.astype(BF16)
            ext = jnp.zeros((LANES, prep_rows), F32)
            for hh in range(2):
                f2 = f_ref[0, 0, hh:hh + 1, pl.ds(r0, prep_rows)] * (-LOG2E)
                for j, piece in enumerate(_bf16_pieces(f2)):
                    ext = jnp.where(rowt == hh * N_PIECES + j, piece, ext)
            kx_ref[pl.ds(r0, prep_rows), LANES:2 * LANES] = ext.T.astype(BF16)
            vt = v_ref[0, pl.ds(r0, prep_rows), :].astype(F32).T
            ones = jnp.ones((ONES_ROWS, prep_rows), BF16)
            for hh in range(2):
                vt_ref[hh, 0:half, pl.ds(r0, prep_rows)] = vt[hh * half:(hh + 1) * half].astype(BF16)
                vt_ref[hh, half:, pl.ds(r0, prep_rows)] = ones
            return carry

        lax.fori_loop(0, s_len // prep_rows, body, 0)

    lane = lax.broadcasted_iota(jnp.int32, (tq, LANES), 1)
    qn = _head_norm(q_ref[0].astype(F32), qg_ref[...], lane) * (FOX_HEAD_DIM ** -0.5 * LOG2E)
    qt = qn.T
    rowq = lax.broadcasted_iota(jnp.int32, (LANES, tq), 0)
    qt_ref[0:LANES, 0:tq] = jnp.where(rowq < half, qt, 0.0).astype(BF16)
    qt_ref[0:LANES, tq:2 * tq] = jnp.where(rowq < half, 0.0, qt).astype(BF16)
    qt_ref[LANES:2 * LANES, 0:tq] = jnp.where(rowq < N_PIECES, 1.0, 0.0).astype(BF16)
    qt_ref[LANES:2 * LANES, tq:2 * tq] = jnp.where(
        (rowq >= N_PIECES) & (rowq < 2 * N_PIECES), 1.0, 0.0).astype(BF16)
    m_ref[...] = jnp.full(m_ref.shape, NEG_BIG, F32)
    acc_ref[...] = jnp.zeros(acc_ref.shape, F32)

    nqh = tq // tu
    units = [(hh, qh) for hh in range(2) for qh in range(nqh)]
    bufs = (sta_ref, stb_ref)

    def logits(c0, unit):
        col = unit[0] * tq + unit[1] * tu
        return jnp.dot(kx_ref[pl.ds(c0, tu), :], qt_ref[:, col:col + tu],
                       preferred_element_type=F32)

    def accumulate(st_ref, c0, unit, triangular):
        st = st_ref[...]
        if triangular:
            key = lax.broadcasted_iota(jnp.int32, (tu, tu), 0)
            qry = lax.broadcasted_iota(jnp.int32, (tu, tu), 1)
            st = jnp.where(key <= qry, st, NEG_BIG)
        col = unit[0] * tq + unit[1] * tu
        cols = slice(col, col + tu)
        m_old = m_ref[:, cols]
        m_new = jnp.maximum(m_old, jnp.max(st, axis=0, keepdims=True))
        alpha = jnp.exp2(m_old - m_new)
        pt = jnp.exp2(st - m_new).astype(BF16)
        m_ref[:, cols] = m_new
        pv = jnp.dot(vt_ref[unit[0], :, pl.ds(c0, tu)], pt, preferred_element_type=F32)
        acc_ref[:, cols] = acc_ref[:, cols] * alpha + pv

    sta_ref[...] = logits(0, units[0])

    def body(kb, carry):
        c0 = pl.multiple_of(kb * tu, tu)
        for i, unit in enumerate(units):
            if i + 1 < len(units):
                bufs[(i + 1) % 2][...] = logits(c0, units[i + 1])
            else:
                bufs[(i + 1) % 2][...] = logits(c0 + tu, units[0])
            accumulate(bufs[i % 2], c0, unit, False)
        return carry

    def body_pair(j, carry):
        return body(2 * j + 1, body(2 * j, carry))

    n_off = qi * nqh
    n_pairs = lax.shift_right_logical(n_off, 1)
    lax.fori_loop(0, n_pairs, body_pair, 0)
    lax.fori_loop(2 * n_pairs, n_off, body, 0)

    cd = pl.multiple_of(qi * tq, tq)
    diag = [(j, (hh, qh)) for j in range(nqh) for hh in range(2) for qh in range(j, nqh)]
    for i, (j, unit) in enumerate(diag):
        if i + 1 < len(diag):
            jn, un = diag[i + 1]
            bufs[(i + 1) % 2][...] = logits(cd + jn * tu, un)
        accumulate(bufs[i % 2], cd + j * tu, unit, unit[1] == j)

    res = []
    for hh in range(2):
        cols = slice(hh * tq, (hh + 1) * tq)
        denom = acc_ref[half:half + SUBLANES, cols]
        res.append(acc_ref[0:half, cols] / jnp.concatenate([denom] * (half // SUBLANES), axis=0))
    res = jnp.concatenate(res, axis=0)
    out_ref[0] = (res.T * _sigmoid(o_ref[0].astype(F32))).astype(BF16)


def _attention(qkvo, f_pairs, qg2, kg2, tq):
    bsz, s, n = qkvo.shape
    d = n // 4
    npairs = d // LANES
    tu = 512
    kern = functools.partial(_attn_kernel, tq=tq, tu=tu, prep_rows=tu)
    return pl.pallas_call(
        kern,
        grid=(bsz, npairs, s // tq),
        in_specs=[
            pl.BlockSpec((1, tq, LANES), lambda b, h, i: (b, i, h)),
            pl.BlockSpec((1, s, LANES), lambda b, h, i: (b, 0, npairs + h)),
            pl.BlockSpec((1, s, LANES), lambda b, h, i: (b, 0, 2 * npairs + h)),
            pl.BlockSpec((1, tq, LANES), lambda b, h, i: (b, i, 3 * npairs + h)),
            pl.BlockSpec((1, 1, 2, s), lambda b, h, i: (b, h, 0, 0)),
            _const_spec((1, LANES)),
            _const_spec((1, LANES)),
        ],
        out_specs=pl.BlockSpec((1, tq, LANES), lambda b, h, i: (b, i, h)),
        out_shape=jax.ShapeDtypeStruct((bsz, s, d), BF16),
        scratch_shapes=[
            pltpu.VMEM((s, 2 * LANES), BF16),
            pltpu.VMEM((2, FOX_HEAD_DIM + ONES_ROWS, s), BF16),
            pltpu.VMEM((2 * LANES, 2 * tq), BF16),
            pltpu.VMEM((1, 2 * tq), F32),
            pltpu.VMEM((FOX_HEAD_DIM + ONES_ROWS, 2 * tq), F32),
            pltpu.VMEM((tu, tu), F32),
            pltpu.VMEM((tu, tu), F32),
        ],
        compiler_params=_cparams(("parallel", "parallel", "arbitrary")),
        name="fox_attn",
    )(qkvo, qkvo, qkvo, qkvo, f_pairs, qg2, kg2)


def _shift_rows(a, d, prev_tail):
    r = pltpu.roll(a, d, axis=0)
    row = lax.broadcasted_iota(jnp.int32, prev_tail.shape, 0)
    top = jnp.where(row < d, pltpu.roll(prev_tail, d, axis=0), r[:SUBLANES])
    return jnp.concatenate([top, r[SUBLANES:]], axis=0)


def _ffn_kernel(*refs, cw, fuse_attn_out, final_norm):
    refs = list(refs)
    x_ref = refs.pop(0)
    if fuse_attn_out:
        ao_ref, wo_ref, g1_ref = refs.pop(0), refs.pop(0), refs.pop(0)
    g_ref, sh_ref, sc_ref, g2_ref, wup_ref, cw_ref, cb_ref, wdn_ref = refs[:8]
    refs = refs[8:]
    if final_norm:
        fg_ref = refs.pop(0)
    o_ref, tail_ref, act_ref = refs

    si = pl.program_id(1)
    tm = x_ref.shape[1]
    dff = wdn_ref.shape[0]

    @pl.when(si == 0)
    def _zero_tail():
        tail_ref[...] = jnp.zeros_like(tail_ref)

    x = x_ref[0]
    if fuse_attn_out:
        x = x + g1_ref[0] * jnp.dot(ao_ref[0], wo_ref[...], preferred_element_type=F32)
    hb = _norm_mod(x, g_ref[...], sh_ref[0], sc_ref[0]).astype(BF16)

    def conv(c0):
        a = _dot_row_halves(hb, wup_ref[:, c0:c0 + cw])
        prev = tail_ref[:, c0:c0 + cw]
        tail_ref[:, c0:c0 + cw] = a[tm - SUBLANES:]
        w = cw_ref[:, c0:c0 + cw]
        return (_shift_rows(a, 2, prev) * w[0:1] + _shift_rows(a, 1, prev) * w[1:2]
                + a * w[2:3] + cb_ref[:, c0:c0 + cw])

    for c0 in range(0, dff, cw):
        gate = conv(c0)
        val = conv(dff + c0)
        act_ref[:, c0:c0 + cw] = (gate * _sigmoid(gate) * val).astype(BF16)

    y = jnp.dot(act_ref[...], wdn_ref[...], preferred_element_type=F32)
    xo = x + g2_ref[0] * y
    if final_norm:
        ms = jnp.mean(xo * xo, axis=-1, keepdims=True)
        xo = xo * lax.rsqrt(ms + EPS) * fg_ref[...]
    o_ref[0] = xo


def _ffn(x, attn_out, norm_g, sh, sc, g2, w_up, conv_w, conv_b, w_down, final_g, tm):
    bsz, s, d = x.shape
    dff = w_down.shape[0]
    fuse_attn_out = attn_out is not None
    final_norm = final_g is not None
    row_spec = pl.BlockSpec((1, tm, d), lambda b, i: (b, i, 0))
    vec_spec = pl.BlockSpec((1, 1, d), lambda b, i: (b, 0, 0))
    args, specs = [x], [row_spec]
    if fuse_attn_out:
        ao, w_out, g1 = attn_out
        args += [ao, w_out, g1]
        specs += [row_spec, _const_spec((d, d)), vec_spec]
    args += [norm_g, sh, sc, g2, w_up, conv_w, conv_b, w_down]
    specs += [_const_spec((1, d)), vec_spec, vec_spec, vec_spec,
              _const_spec((d, 2 * dff)), _const_spec((CONV_WIDTH, 2 * dff)),
              _const_spec((1, 2 * dff)), _const_spec((dff, d))]
    if final_norm:
        args.append(final_g)
        specs.append(_const_spec((1, d)))
    kern = functools.partial(_ffn_kernel, cw=256, fuse_attn_out=fuse_attn_out,
                             final_norm=final_norm)
    return pl.pallas_call(
        kern,
        grid=(bsz, s // tm),
        in_specs=specs,
        out_specs=row_spec,
        out_shape=jax.ShapeDtypeStruct((bsz, s, d), F32),
        scratch_shapes=[
            pltpu.VMEM((SUBLANES, 2 * dff), F32),
            pltpu.VMEM((tm, dff), BF16),
        ],
        compiler_params=_cparams(("parallel", "arbitrary")),
        name="conv_ffn",
    )(*args)


GELU_C = 0.7978845608028654
GELU_A = 0.044715


def _gelu_tanh(x):
    inner = x * (GELU_C + (GELU_C * GELU_A) * (x * x))
    hx = 0.5 * x
    return hx + hx * jnp.tanh(inner)


def _sgu_kernel(x_ref, g_ref, sh_ref, sc_ref, g1_ref, win_ref, bin_ref, vg_ref, vb_ref,
                ws_ref, bs_ref, wout_ref, o_ref, v_ref, vn_ref, y_ref, u_ref):
    tm = x_ref.shape[1]
    width = vg_ref.shape[1]
    gdim = width // SGU_GROUPS
    x = x_ref[0]
    hb = _norm_mod(x, g_ref[...], sh_ref[0], sc_ref[0]).astype(BF16)

    def z_cols(c0, n):
        return _gelu_tanh(_dot_row_halves(hb, win_ref[:, c0:c0 + n]) + bin_ref[:, c0:c0 + n])

    vcw = 512
    for c0 in range(0, width, vcw):
        v_ref[:, c0:c0 + vcw] = z_cols(width + c0, vcw)
    for c0 in range(0, width, vcw):
        u_ref[:, c0:c0 + vcw] = z_cols(c0, vcw)
    v = v_ref[...]
    mu = jnp.mean(v, axis=-1, keepdims=True)
    dv = v - mu
    var = jnp.mean(dv * dv, axis=-1, keepdims=True)
    vn_ref[...] = (dv * lax.rsqrt(var + EPS) * vg_ref[...] + vb_ref[...]).astype(BF16)

    t = lax.broadcasted_iota(jnp.int32, (SGU_BLOCK, SGU_BLOCK), 0)
    s = lax.broadcasted_iota(jnp.int32, (SGU_BLOCK, SGU_BLOCK), 1)
    allowed = (s // CHUNK) <= (t // CHUNK)
    for g in range(SGU_GROUPS):
        c0 = g * gdim
        ws = jnp.where(allowed, ws_ref[g], 0.0).astype(BF16)
        bias = jnp.concatenate([bs_ref[g]] * (gdim // LANES), axis=1)
        for r0 in range(0, tm, SGU_BLOCK):
            mixed = jnp.dot(ws, vn_ref[r0:r0 + SGU_BLOCK, c0:c0 + gdim],
                            preferred_element_type=F32) + bias
            y_ref[r0:r0 + SGU_BLOCK, c0:c0 + gdim] = (
                u_ref[r0:r0 + SGU_BLOCK, c0:c0 + gdim] * mixed).astype(BF16)

    y = jnp.dot(y_ref[...], wout_ref[...], preferred_element_type=F32)
    o_ref[0] = x + g1_ref[0] * y


def _sgu(x, norm_g, sh, sc, g1, w_in, b_in, v_gain, v_bias, w_s, bs_b, w_out, tm):
    bsz, s, d = x.shape
    width = w_out.shape[0]
    row_spec = pl.BlockSpec((1, tm, d), lambda b, i: (b, i, 0))
    vec_spec = pl.BlockSpec((1, 1, d), lambda b, i: (b, 0, 0))
    return pl.pallas_call(
        _sgu_kernel,
        grid=(bsz, s // tm),
        in_specs=[
            row_spec, _const_spec((1, d)), vec_spec, vec_spec, vec_spec,
            _const_spec((d, 2 * width)), _const_spec((1, 2 * width)),
            _const_spec((1, width)), _const_spec((1, width)),
            _const_spec((SGU_GROUPS, SGU_BLOCK, SGU_BLOCK)),
            _const_spec((SGU_GROUPS, SGU_BLOCK, LANES)),
            _const_spec((width, d)),
        ],
        out_specs=row_spec,
        out_shape=jax.ShapeDtypeStruct((bsz, s, d), F32),
        scratch_shapes=[
            pltpu.VMEM((tm, width), F32),
            pltpu.VMEM((tm, width), BF16),
            pltpu.VMEM((tm, width), BF16),
            pltpu.VMEM((tm, width), F32),
        ],
        compiler_params=_cparams(("parallel", "parallel")),
        name="sgu",
    )(x, norm_g, sh, sc, g1, w_in, b_in, v_gain, v_bias, w_s, bs_b, w_out)


def kernel(x, c, fox_w_in, fox_b_f, fox_q_gain, fox_k_gain, fox_w_out, sgu_w_in, sgu_b_in,
           sgu_v_gain, sgu_v_bias, sgu_w_s, sgu_b_s, sgu_w_out, ffn_w_up, ffn_conv_w,
           ffn_conv_b, ffn_w_down, ada_w, ada_b, norm1_g, norm2_g, final_g):
    bsz, s, d = x.shape
    depth = ada_w.shape[0]
    assert depth == 2 and d == FOX_HEADS * FOX_HEAD_DIM
    assert s % 512 == 0
    tm = 512
    tq = 2048 if s % 2048 == 0 else 512

    mod = _ada_mod(c, ada_w, ada_b)
    mods = [[mod[l, :, k * d:(k + 1) * d].reshape(bsz, 1, d) for k in range(6)]
            for l in range(depth)]

    sh1, sc1, g1, sh2, sc2, g2 = mods[0]
    w_in = fox_w_in[0]
    w_main = w_in[:, :4 * d].astype(BF16)
    w_fl = jnp.zeros((LANES, d), BF16).at[:FOX_HEADS].set(w_in[:, 4 * d:].T.astype(BF16))
    qkvo, fl = _fox_in(x, norm1_g[0].reshape(1, d), sh1, sc1, w_main, w_fl, tm)

    fl_t = fl.reshape(bsz * FOX_HEADS, s // LANES, LANES)
    bf_rows = jnp.broadcast_to(jnp.tile(fox_b_f[0], bsz)[:, None, None],
                               (bsz * FOX_HEADS, 1, LANES))
    f_pairs = _fcum(fl_t, bf_rows, FOX_HEADS).reshape(bsz, FOX_HEADS // 2, 2, s)

    qg2 = jnp.tile(fox_q_gain[0], 2).reshape(1, LANES)
    kg2 = jnp.tile(fox_k_gain[0], 2).reshape(1, LANES)
    ao = _attention(qkvo, f_pairs, qg2, kg2, tq)

    x = _ffn(x, (ao, fox_w_out[0].astype(BF16), g1), norm2_g[0].reshape(1, d), sh2, sc2, g2,
             ffn_w_up[0].astype(BF16), ffn_conv_w[0], ffn_conv_b[0].reshape(1, -1),
             ffn_w_down[0].astype(BF16), None, tm)

    sh1, sc1, g1, sh2, sc2, g2 = mods[1]
    width = sgu_w_out.shape[1]
    bs_b = jnp.broadcast_to(sgu_b_s[0][:, :, None], (SGU_GROUPS, SGU_BLOCK, LANES))
    x = _sgu(x, norm1_g[1].reshape(1, d), sh1, sc1, g1, sgu_w_in[0].astype(BF16),
             sgu_b_in[0].reshape(1, -1), sgu_v_gain[0].reshape(1, width),
             sgu_v_bias[0].reshape(1, width), sgu_w_s[0], bs_b, sgu_w_out[0].astype(BF16), tm)
    x = _ffn(x, None, norm2_g[1].reshape(1, d), sh2, sc2, g2,
             ffn_w_up[1].astype(BF16), ffn_conv_w[1], ffn_conv_b[1].reshape(1, -1),
             ffn_w_down[1].astype(BF16), final_g.reshape(1, d), tm)
    return x
```

```python
import functools

import jax
import jax.numpy as jnp
from jax import lax
from jax.experimental import pallas as pl
from jax.experimental.pallas import tpu as pltpu

F32 = jnp.float32
BF16 = jnp.bfloat16

FOX_HEADS = 16
FOX_HEAD_DIM = 64
SGU_GROUPS = 8
SGU_BLOCK = 128
CHUNK = 64
CONV_WIDTH = 3
EPS = 1e-6

LANES = 128
SUBLANES = 8
VMEM_LIMIT = 56 * 1024 * 1024
NEG_BIG = -1e30


def _cparams(sem):
    return pltpu.CompilerParams(dimension_semantics=sem, vmem_limit_bytes=VMEM_LIMIT)


def _const_spec(shape):
    nd = len(shape)
    return pl.BlockSpec(shape, lambda *_: (0,) * nd, pipeline_mode=pl.Buffered(1))


def _sigmoid(x):
    return 1.0 / (1.0 + jnp.exp(-x))


def _dot_row_halves(a, w):
    hm = a.shape[0] // 2
    return jnp.concatenate([jnp.dot(a[:hm], w, preferred_element_type=F32),
                            jnp.dot(a[hm:], w, preferred_element_type=F32)], axis=0)


def _norm_mod(x, g, sh, sc):
    ms = jnp.mean(x * x, axis=-1, keepdims=True)
    h = x * lax.rsqrt(ms + EPS) * g
    return h * (1.0 + sc) + sh


def _mod_kernel(c_ref, w_ref, b_ref, o_ref):
    c = c_ref[...]
    ca = c * _sigmoid(c)
    o_ref[0] = jnp.dot(ca, w_ref[0], preferred_element_type=F32,
                       precision=lax.Precision.HIGHEST) + b_ref[0]


def _ada_mod(c, ada_w, ada_b):
    depth, d, n = ada_w.shape
    bsz = c.shape[0]
    rows = -(-bsz // SUBLANES) * SUBLANES
    cp = jnp.zeros((rows, d), F32).at[:bsz].set(c)
    tn = 1024
    out = pl.pallas_call(
        _mod_kernel,
        grid=(depth, n // tn),
        in_specs=[
            pl.BlockSpec((rows, d), lambda l, j: (0, 0)),
            pl.BlockSpec((1, d, tn), lambda l, j: (l, 0, j)),
            pl.BlockSpec((1, 1, tn), lambda l, j: (l, 0, j)),
        ],
        out_specs=pl.BlockSpec((1, rows, tn), lambda l, j: (l, 0, j)),
        out_shape=jax.ShapeDtypeStruct((depth, rows, n), F32),
        compiler_params=_cparams(("parallel", "parallel")),
        name="ada_mod",
    )(cp, ada_w, ada_b.reshape(depth, 1, n))
    return out[:, :bsz]


def _fox_in_kernel(x_ref, g_ref, sh_ref, sc_ref, w_ref, wfl_ref, o_ref, fl_ref, *, tn):
    hb = _norm_mod(x_ref[0], g_ref[...], sh_ref[0], sc_ref[0]).astype(BF16)
    n = w_ref.shape[1]
    for j in range(0, n, tn):
        o_ref[0, :, j:j + tn] = jnp.dot(
            hb, w_ref[:, j:j + tn], preferred_element_type=F32).astype(BF16)
    flt = lax.dot_general(wfl_ref[...], hb, (((1,), (1,)), ((), ())),
                          preferred_element_type=F32)
    fl_ref[0] = flt[0:fl_ref.shape[1]]


def _fox_in(x, g, sh, sc, w, wfl, tm):
    bsz, s, d = x.shape
    n = w.shape[1]
    kern = functools.partial(_fox_in_kernel, tn=1024)
    return pl.pallas_call(
        kern,
        grid=(bsz, s // tm),
        in_specs=[
            pl.BlockSpec((1, tm, d), lambda b, i: (b, i, 0)),
            _const_spec((1, d)),
            pl.BlockSpec((1, 1, d), lambda b, i: (b, 0, 0)),
            pl.BlockSpec((1, 1, d), lambda b, i: (b, 0, 0)),
            _const_spec((d, n)),
            _const_spec((LANES, d)),
        ],
        out_specs=[
            pl.BlockSpec((1, tm, n), lambda b, i: (b, i, 0)),
            pl.BlockSpec((1, FOX_HEADS, tm), lambda b, i: (b, 0, i)),
        ],
        out_shape=[
            jax.ShapeDtypeStruct((bsz, s, n), BF16),
            jax.ShapeDtypeStruct((bsz, FOX_HEADS, s), F32),
        ],
        compiler_params=_cparams(("parallel", "parallel")),
        name="fox_in",
    )(x, g, sh, sc, w, wfl)


def _fcum_kernel(fl_ref, bf_ref, o_ref):
    r = fl_ref.shape[1]
    lane = lax.broadcasted_iota(jnp.int32, (r, LANES), 1)
    row = lax.broadcasted_iota(jnp.int32, (r, LANES), 0)
    for h in range(fl_ref.shape[0]):
        z = fl_ref[h] + bf_ref[h]
        v = jnp.minimum(z, 0.0) - jnp.log(1.0 + jnp.exp(-jnp.abs(z)))
        d = 1
        while d < LANES:
            v = v + jnp.where(lane >= d, pltpu.roll(v, d, axis=1), 0.0)
            d *= 2
        tot = jnp.broadcast_to(v[:, LANES - 1:LANES], v.shape)
        inc = tot
        d = 1
        while d < r:
            inc = inc + jnp.where(row >= d, pltpu.roll(inc, d, axis=0), 0.0)
            d *= 2
        o_ref[h] = v + (inc - tot)


def _fcum(fl_t, bf_rows, heads):
    bh, r, _ = fl_t.shape
    return pl.pallas_call(
        _fcum_kernel,
        grid=(bh // heads,),
        in_specs=[
            pl.BlockSpec((heads, r, LANES), lambda i: (i, 0, 0)),
            pl.BlockSpec((heads, 1, LANES), lambda i: (i, 0, 0)),
        ],
        out_specs=pl.BlockSpec((heads, r, LANES), lambda i: (i, 0, 0)),
        out_shape=jax.ShapeDtypeStruct((bh, r, LANES), F32),
        compiler_params=_cparams(("parallel",)),
        name="fcum",
    )(fl_t, bf_rows)


LOG2E = 1.4426950408889634
N_PIECES = 3
ONES_ROWS = 16


def _bf16_pieces(x):
    out = []
    for _ in range(N_PIECES):
        p = x.astype(BF16).astype(F32)
        out.append(p)
        x = x - p
    return out


def _head_norm(x, gain, lane):
    x2 = x * x
    lo = lane < FOX_HEAD_DIM
    ss_a = jnp.sum(jnp.where(lo, x2, 0.0), axis=-1, keepdims=True)
    ss_b = jnp.sum(jnp.where(lo, 0.0, x2), axis=-1, keepdims=True)
    ms = jnp.where(lo, ss_a, ss_b) * (1.0 / FOX_HEAD_DIM)
    return x * lax.rsqrt(ms + EPS) * gain


def _attn_kernel(q_ref, k_ref, v_ref, o_ref, f_ref, qg_ref, kg_ref, out_ref,
                 kx_ref, vt_ref, qt_ref, m_ref, acc_ref, sta_ref, stb_ref, *, tq, tu, prep_rows):
    qi = pl.program_id(2)
    s_len = k_ref.shape[1]
    half = FOX_HEAD_DIM

    @pl.when(qi == 0)
    def _prepare_kv():
        lane = lax.broadcasted_iota(jnp.int32, (prep_rows, LANES), 1)
        rowt = lax.broadcasted_iota(jnp.int32, (LANES, prep_rows), 0)

        def body(c, carry):
            r0 = pl.multiple_of(c * prep_rows, prep_rows)
            kn = _head_norm(k_ref[0, pl.ds(r0, prep_rows), :].astype(F32), kg_ref[...], lane)
            kx_ref[pl.ds(r0, prep_rows), 0:LANES] = kn.astype(BF16)
            ext = jnp.zeros((LANES, prep_rows), F32)
            for hh in range(2):
                f2 = f_ref[0, 0, hh:hh + 1, pl.ds(r0, prep_rows)] * (-LOG2E)
                for j, piece in enumerate(_bf16_pieces(f2)):
                    ext = jnp.where(rowt == hh * N_PIECES + j, piece, ext)
            kx_ref[pl.ds(r0, prep_rows), LANES:2 * LANES] = ext.T.astype(BF16)
            vt = v_ref[0, pl.ds(r0, prep_rows), :].astype(F32).T
            ones = jnp.ones((ONES_ROWS, prep_rows), BF16)
            for hh in range(2):
                vt_ref[hh, 0:half, pl.ds(r0, prep_rows)] = vt[hh * half:(hh + 1) * half].astype(BF16)
                vt_ref[hh, half:, pl.ds(r0, prep_rows)] = ones
            return carry

        lax.fori_loop(0, s_len // prep_rows, body, 0)

    lane = lax.broadcasted_iota(jnp.int32, (tq, LANES), 1)
    qn = _head_norm(q_ref[0].astype(F32), qg_ref[...], lane) * (FOX_HEAD_DIM ** -0.5 * LOG2E)
    qt = qn.T
    rowq = lax.broadcasted_iota(jnp.int32, (LANES, tq), 0)
    qt_ref[0:LANES, 0:tq] = jnp.where(rowq < half, qt, 0.0).astype(BF16)
    qt_ref[0:LANES, tq:2 * tq] = jnp.where(rowq < half, 0.0, qt).astype(BF16)
    qt_ref[LANES:2 * LANES, 0:tq] = jnp.where(rowq < N_PIECES, 1.0, 0.0).astype(BF16)
    qt_ref[LANES:2 * LANES, tq:2 * tq] = jnp.where(
        (rowq >= N_PIECES) & (rowq < 2 * N_PIECES), 1.0, 0.0).astype(BF16)
    m_ref[...] = jnp.full(m_ref.shape, NEG_BIG, F32)
    acc_ref[...] = jnp.zeros(acc_ref.shape, F32)

    nqh = tq // tu
    units = [(hh, qh) for hh in range(2) for qh in range(nqh)]
    bufs = (sta_ref, stb_ref)

    def logits(c0, unit):
        col = unit[0] * tq + unit[1] * tu
        return _dot_row_halves(kx_ref[pl.ds(c0, tu), :], qt_ref[:, col:col + tu])

    def accumulate(st_ref, c0, unit, triangular):
        st = st_ref[...]
        if triangular:
            key = lax.broadcasted_iota(jnp.int32, (tu, tu), 0)
            qry = lax.broadcasted_iota(jnp.int32, (tu, tu), 1)
            st = jnp.where(key <= qry, st, NEG_BIG)
        col = unit[0] * tq + unit[1] * tu
        cols = slice(col, col + tu)
        m_old = m_ref[:, cols]
        m_new = jnp.maximum(m_old, jnp.max(st, axis=0, keepdims=True))
        alpha = jnp.exp2(m_old - m_new)
        pt = jnp.exp2(st - m_new).astype(BF16)
        m_ref[:, cols] = m_new
        pv = jnp.dot(vt_ref[unit[0], :, pl.ds(c0, tu)], pt, preferred_element_type=F32)
        acc_ref[:, cols] = acc_ref[:, cols] * alpha + pv

    sta_ref[...] = logits(0, units[0])

    def body(kb, carry):
        c0 = pl.multiple_of(kb * tu, tu)
        for i, unit in enumerate(units):
            if i + 1 < len(units):
                bufs[(i + 1) % 2][...] = logits(c0, units[i + 1])
            else:
                bufs[(i + 1) % 2][...] = logits(c0 + tu, units[0])
            accumulate(bufs[i % 2], c0, unit, False)
        return carry

    def body_pair(j, carry):
        return body(2 * j + 1, body(2 * j, carry))

    n_off = qi * nqh
    n_pairs = lax.shift_right_logical(n_off, 1)
    lax.fori_loop(0, n_pairs, body_pair, 0)
    lax.fori_loop(2 * n_pairs, n_off, body, 0)

    cd = pl.multiple_of(qi * tq, tq)
    diag = [(j, (hh, qh)) for j in range(nqh) for hh in range(2) for qh in range(j, nqh)]
    for i, (j, unit) in enumerate(diag):
        if i + 1 < len(diag):
            jn, un = diag[i + 1]
            bufs[(i + 1) % 2][...] = logits(cd + jn * tu, un)
        accumulate(bufs[i % 2], cd + j * tu, unit, unit[1] == j)

    res = []
    for hh in range(2):
        cols = slice(hh * tq, (hh + 1) * tq)
        denom = acc_ref[half:half + SUBLANES, cols]
        res.append(acc_ref[0:half, cols] / jnp.concatenate([denom] * (half // SUBLANES), axis=0))
    res = jnp.concatenate(res, axis=0)
    out_ref[0] = (res.T * _sigmoid(o_ref[0].astype(F32))).astype(BF16)


def _attention(qkvo, f_pairs, qg2, kg2, tq):
    bsz, s, n = qkvo.shape
    d = n // 4
    npairs = d // LANES
    tu = 512
    kern = functools.partial(_attn_kernel, tq=tq, tu=tu, prep_rows=tu)
    return pl.pallas_call(
        kern,
        grid=(bsz, npairs, s // tq),
        in_specs=[
            pl.BlockSpec((1, tq, LANES), lambda b, h, i: (b, i, h)),
            pl.BlockSpec((1, s, LANES), lambda b, h, i: (b, 0, npairs + h)),
            pl.BlockSpec((1, s, LANES), lambda b, h, i: (b, 0, 2 * npairs + h)),
            pl.BlockSpec((1, tq, LANES), lambda b, h, i: (b, i, 3 * npairs + h)),
            pl.BlockSpec((1, 1, 2, s), lambda b, h, i: (b, h, 0, 0)),
            _const_spec((1, LANES)),
            _const_spec((1, LANES)),
        ],
        out_specs=pl.BlockSpec((1, tq, LANES), lambda b, h, i: (b, i, h)),
        out_shape=jax.ShapeDtypeStruct((bsz, s, d), BF16),
        scratch_shapes=[
            pltpu.VMEM((s, 2 * LANES), BF16),
            pltpu.VMEM((2, FOX_HEAD_DIM + ONES_ROWS, s), BF16),
            pltpu.VMEM((2 * LANES, 2 * tq), BF16),
            pltpu.VMEM((1, 2 * tq), F32),
            pltpu.VMEM((FOX_HEAD_DIM + ONES_ROWS, 2 * tq), F32),
            pltpu.VMEM((tu, tu), F32),
            pltpu.VMEM((tu, tu), F32),
        ],
        compiler_params=_cparams(("parallel", "parallel", "arbitrary")),
        name="fox_attn",
    )(qkvo, qkvo, qkvo, qkvo, f_pairs, qg2, kg2)


def _shift_rows(a, d, prev_tail):
    r = pltpu.roll(a, d, axis=0)
    row = lax.broadcasted_iota(jnp.int32, prev_tail.shape, 0)
    top = jnp.where(row < d, pltpu.roll(prev_tail, d, axis=0), r[:SUBLANES])
    return jnp.concatenate([top, r[SUBLANES:]], axis=0)


def _ffn_kernel(*refs, cw, fuse_attn_out, final_norm):
    refs = list(refs)
    x_ref = refs.pop(0)
    if fuse_attn_out:
        ao_ref, wo_ref, g1_ref = refs.pop(0), refs.pop(0), refs.pop(0)
    g_ref, sh_ref, sc_ref, g2_ref, wup_ref, cw_ref, cb_ref, wdn_ref = refs[:8]
    refs = refs[8:]
    if final_norm:
        fg_ref = refs.pop(0)
    o_ref, tail_ref, act_ref = refs

    si = pl.program_id(1)
    tm = x_ref.shape[1]
    dff = wdn_ref.shape[0]

    @pl.when(si == 0)
    def _zero_tail():
        tail_ref[...] = jnp.zeros_like(tail_ref)

    x = x_ref[0]
    if fuse_attn_out:
        x = x + g1_ref[0] * jnp.dot(ao_ref[0], wo_ref[...], preferred_element_type=F32)
    hb = _norm_mod(x, g_ref[...], sh_ref[0], sc_ref[0]).astype(BF16)

    def conv(c0):
        a = _dot_row_halves(hb, wup_ref[:, c0:c0 + cw])
        prev = tail_ref[:, c0:c0 + cw]
        tail_ref[:, c0:c0 + cw] = a[tm - SUBLANES:]
        w = cw_ref[:, c0:c0 + cw]
        return (_shift_rows(a, 2, prev) * w[0:1] + _shift_rows(a, 1, prev) * w[1:2]
                + a * w[2:3] + cb_ref[:, c0:c0 + cw])

    for c0 in range(0, dff, cw):
        gate = conv(c0)
        val = conv(dff + c0)
        act_ref[:, c0:c0 + cw] = (gate * _sigmoid(gate) * val).astype(BF16)

    y = jnp.dot(act_ref[...], wdn_ref[...], preferred_element_type=F32)
    xo = x + g2_ref[0] * y
    if final_norm:
        ms = jnp.mean(xo * xo, axis=-1, keepdims=True)
        xo = xo * lax.rsqrt(ms + EPS) * fg_ref[...]
    o_ref[0] = xo


def _ffn(x, attn_out, norm_g, sh, sc, g2, w_up, conv_w, conv_b, w_down, final_g, tm):
    bsz, s, d = x.shape
    dff = w_down.shape[0]
    fuse_attn_out = attn_out is not None
    final_norm = final_g is not None
    row_spec = pl.BlockSpec((1, tm, d), lambda b, i: (b, i, 0))
    vec_spec = pl.BlockSpec((1, 1, d), lambda b, i: (b, 0, 0))
    args, specs = [x], [row_spec]
    if fuse_attn_out:
        ao, w_out, g1 = attn_out
        args += [ao, w_out, g1]
        specs += [row_spec, _const_spec((d, d)), vec_spec]
    args += [norm_g, sh, sc, g2, w_up, conv_w, conv_b, w_down]
    specs += [_const_spec((1, d)), vec_spec, vec_spec, vec_spec,
              _const_spec((d, 2 * dff)), _const_spec((CONV_WIDTH, 2 * dff)),
              _const_spec((1, 2 * dff)), _const_spec((dff, d))]
    if final_norm:
        args.append(final_g)
        specs.append(_const_spec((1, d)))
    kern = functools.partial(_ffn_kernel, cw=256, fuse_attn_out=fuse_attn_out,
                             final_norm=final_norm)
    return pl.pallas_call(
        kern,
        grid=(bsz, s // tm),
        in_specs=specs,
        out_specs=row_spec,
        out_shape=jax.ShapeDtypeStruct((bsz, s, d), F32),
        scratch_shapes=[
            pltpu.VMEM((SUBLANES, 2 * dff), F32),
            pltpu.VMEM((tm, dff), BF16),
        ],
        compiler_params=_cparams(("parallel", "arbitrary")),
        name="conv_ffn",
    )(*args)


GELU_C = 0.7978845608028654
GELU_A = 0.044715


def _gelu_tanh(x):
    inner = x * (GELU_C + (GELU_C * GELU_A) * (x * x))
    hx = 0.5 * x
    return hx + hx * jnp.tanh(inner)


def _sgu_kernel(x_ref, g_ref, sh_ref, sc_ref, g1_ref, win_ref, bin_ref, vg_ref, vb_ref,
                ws_ref, bs_ref, wout_ref, o_ref, v_ref, vn_ref, y_ref, u_ref):
    tm = x_ref.shape[1]
    width = vg_ref.shape[1]
    gdim = width // SGU_GROUPS
    x = x_ref[0]
    hb = _norm_mod(x, g_ref[...], sh_ref[0], sc_ref[0]).astype(BF16)

    def z_cols(c0, n):
        return _gelu_tanh(_dot_row_halves(hb, win_ref[:, c0:c0 + n]) + bin_ref[:, c0:c0 + n])

    vcw = 512
    for c0 in range(0, width, vcw):
        v_ref[:, c0:c0 + vcw] = z_cols(width + c0, vcw)
    for c0 in range(0, width, vcw):
        u_ref[:, c0:c0 + vcw] = z_cols(c0, vcw)
    v = v_ref[...]
    mu = jnp.mean(v, axis=-1, keepdims=True)
    dv = v - mu
    var = jnp.mean(dv * dv, axis=-1, keepdims=True)
    vn_ref[...] = (dv * lax.rsqrt(var + EPS) * vg_ref[...] + vb_ref[...]).astype(BF16)

    t = lax.broadcasted_iota(jnp.int32, (SGU_BLOCK, SGU_BLOCK), 0)
    s = lax.broadcasted_iota(jnp.int32, (SGU_BLOCK, SGU_BLOCK), 1)
    allowed = (s // CHUNK) <= (t // CHUNK)
    for g in range(SGU_GROUPS):
        c0 = g * gdim
        ws = jnp.where(allowed, ws_ref[g], 0.0).astype(BF16)
        bias = jnp.concatenate([bs_ref[g]] * (gdim // LANES), axis=1)
        for r0 in range(0, tm, SGU_BLOCK):
            mixed = jnp.dot(ws, vn_ref[r0:r0 + SGU_BLOCK, c0:c0 + gdim],
                            preferred_element_type=F32) + bias
            y_ref[r0:r0 + SGU_BLOCK, c0:c0 + gdim] = (
                u_ref[r0:r0 + SGU_BLOCK, c0:c0 + gdim] * mixed).astype(BF16)

    y = jnp.dot(y_ref[...], wout_ref[...], preferred_element_type=F32)
    o_ref[0] = x + g1_ref[0] * y


def _sgu(x, norm_g, sh, sc, g1, w_in, b_in, v_gain, v_bias, w_s, bs_b, w_out, tm):
    bsz, s, d = x.shape
    width = w_out.shape[0]
    row_spec = pl.BlockSpec((1, tm, d), lambda b, i: (b, i, 0))
    vec_spec = pl.BlockSpec((1, 1, d), lambda b, i: (b, 0, 0))
    return pl.pallas_call(
        _sgu_kernel,
        grid=(bsz, s // tm),
        in_specs=[
            row_spec, _const_spec((1, d)), vec_spec, vec_spec, vec_spec,
            _const_spec((d, 2 * width)), _const_spec((1, 2 * width)),
            _const_spec((1, width)), _const_spec((1, width)),
            _const_spec((SGU_GROUPS, SGU_BLOCK, SGU_BLOCK)),
            _const_spec((SGU_GROUPS, SGU_BLOCK, LANES)),
            _const_spec((width, d)),
        ],
        out_specs=row_spec,
        out_shape=jax.ShapeDtypeStruct((bsz, s, d), F32),
        scratch_shapes=[
            pltpu.VMEM((tm, width), F32),
            pltpu.VMEM((tm, width), BF16),
            pltpu.VMEM((tm, width), BF16),
            pltpu.VMEM((tm, width), F32),
        ],
        compiler_params=_cparams(("parallel", "parallel")),
        name="sgu",
    )(x, norm_g, sh, sc, g1, w_in, b_in, v_gain, v_bias, w_s, bs_b, w_out)


def kernel(x, c, fox_w_in, fox_b_f, fox_q_gain, fox_k_gain, fox_w_out, sgu_w_in, sgu_b_in,
           sgu_v_gain, sgu_v_bias, sgu_w_s, sgu_b_s, sgu_w_out, ffn_w_up, ffn_conv_w,
           ffn_conv_b, ffn_w_down, ada_w, ada_b, norm1_g, norm2_g, final_g):
    bsz, s, d = x.shape
    depth = ada_w.shape[0]
    assert depth == 2 and d == FOX_HEADS * FOX_HEAD_DIM
    assert s % 512 == 0
    tm = 512
    tq = 2048 if s % 2048 == 0 else 512

    mod = _ada_mod(c, ada_w, ada_b)
    mods = [[mod[l, :, k * d:(k + 1) * d].reshape(bsz, 1, d) for k in range(6)]
            for l in range(depth)]

    sh1, sc1, g1, sh2, sc2, g2 = mods[0]
    w_in = fox_w_in[0]
    w_main = w_in[:, :4 * d].astype(BF16)
    w_fl = jnp.zeros((LANES, d), BF16).at[:FOX_HEADS].set(w_in[:, 4 * d:].T.astype(BF16))
    qkvo, fl = _fox_in(x, norm1_g[0].reshape(1, d), sh1, sc1, w_main, w_fl, tm)

    fl_t = fl.reshape(bsz * FOX_HEADS, s // LANES, LANES)
    bf_rows = jnp.broadcast_to(jnp.tile(fox_b_f[0], bsz)[:, None, None],
                               (bsz * FOX_HEADS, 1, LANES))
    f_pairs = _fcum(fl_t, bf_rows, FOX_HEADS).reshape(bsz, FOX_HEADS // 2, 2, s)

    qg2 = jnp.tile(fox_q_gain[0], 2).reshape(1, LANES)
    kg2 = jnp.tile(fox_k_gain[0], 2).reshape(1, LANES)
    ao = _attention(qkvo, f_pairs, qg2, kg2, tq)

    x = _ffn(x, (ao, fox_w_out[0].astype(BF16), g1), norm2_g[0].reshape(1, d), sh2, sc2, g2,
             ffn_w_up[0].astype(BF16), ffn_conv_w[0], ffn_conv_b[0].reshape(1, -1),
             ffn_w_down[0].astype(BF16), None, tm)

    sh1, sc1, g1, sh2, sc2, g2 = mods[1]
    width = sgu_w_out.shape[1]
    bs_b = jnp.broadcast_to(sgu_b_s[0][:, :, None], (SGU_GROUPS, SGU_BLOCK, LANES))
    x = _sgu(x, norm1_g[1].reshape(1, d), sh1, sc1, g1, sgu_w_in[0].astype(BF16),
             sgu_b_in[0].reshape(1, -1), sgu_v_gain[0].reshape(1, width),
             sgu_v_bias[0].reshape(1, width), sgu_w_s[0], bs_b, sgu_w_out[0].astype(BF16), tm)
    x = _ffn(x, None, norm2_g[1].reshape(1, d), sh2, sc2, g2,
             ffn_w_up[1].astype(BF16), ffn_conv_w[1], ffn_conv_b[1].reshape(1, -1),
             ffn_w_down[1].astype(BF16), final_g.reshape(1, d), tm)
    return x
```
